```python
import math
import numpy as np
import jax, jax.numpy as jnp
from jax import lax

D_MODEL = 1024
BATCH = 4
SEQ = 8192
DEPTH = 2

D_MIX = D_MODEL
RG_WIDTH = D_MIX // 2
RG_BLOCKS = 8
RG_BLOCK = RG_WIDTH // RG_BLOCKS
CONV_W = 4
RG_C = 8.0
A_MIN = 0.9
A_MAX = 0.999

NSA_WIDTH = D_MIX - RG_WIDTH
HEAD_DIM = 64
N_HEADS = NSA_WIDTH // HEAD_DIM
N_KV = 2
HPG = N_HEADS // N_KV
KV_WIDTH = N_KV * HEAD_DIM
ROPE_DIM = HEAD_DIM // 4
ROPE_THETA = 500000.0
CMP_LEN = 32
CMP_STRIDE = 16
CMP_HIDDEN = 2 * HEAD_DIM
SEL_LEN = 64
SEL_TOPN = 16
WINDOW = 512
N_BRANCH = 3
Q_BLOCK = 128
EPS = 1e-6
NEG = -1e30
BIG = 1e30
N_IN = 2 * RG_WIDTH + 2 * NSA_WIDTH + 6 * KV_WIDTH + N_BRANCH * N_HEADS

kernel_name = "hybrid_rglru_nsa_parallel_heads"


def rms_norm(x, g):
    xf = x.astype(jnp.float32)
    y = xf * lax.rsqrt(jnp.mean(xf * xf, axis=-1, keepdims=True) + EPS)
    return (y * g.astype(jnp.float32)).astype(x.dtype)


def rope_partial(x, pos):
    half = ROPE_DIM // 2
    inv = ROPE_THETA ** (-jnp.arange(half, dtype=jnp.float32) / half)
    ang = pos[:, None] * inv[None, :]
    cos, sin = jnp.cos(ang), jnp.sin(ang)
    xf = x.astype(jnp.float32)
    x1, x2, xp = xf[..., :half], xf[..., half:ROPE_DIM], xf[..., ROPE_DIM:]
    out = jnp.concatenate([x1 * cos - x2 * sin, x2 * cos + x1 * sin, xp], axis=-1)
    return out.astype(x.dtype)


def masked_softmax(s, mask):
    s = jnp.where(mask, s.astype(jnp.float32), NEG)
    return jax.nn.softmax(s, axis=-1) * mask


def causal_depthwise_conv(x, w, b):
    S = x.shape[1]
    xp = jnp.pad(x, ((0, 0), (CONV_W - 1, 0), (0, 0)))
    y = sum(xp[:, k:k + S] * w[k] for k in range(CONV_W))
    return y + b


def rg_lru(x, w_r, b_r, w_i, b_i, lam):
    B, S, C = x.shape
    xb = x.reshape(B, S, RG_BLOCKS, RG_BLOCK)
    r = jax.nn.sigmoid(jnp.einsum('bsnc,ncd->bsnd', xb, w_r).reshape(B, S, C) + b_r)
    i = jax.nn.sigmoid(jnp.einsum('bsnc,ncd->bsnd', xb, w_i).reshape(B, S, C) + b_i)
    log_a = (-RG_C * r.astype(jnp.float32)) * jax.nn.softplus(-lam.astype(jnp.float32))
    a = jnp.exp(log_a)
    u = jnp.sqrt(-jnp.expm1(2.0 * log_a)) * (i * x).astype(jnp.float32)

    def combine(left, right):
        a1, b1 = left
        a2, b2 = right
        return a1 * a2, a2 * b1 + b2

    _, h = lax.associative_scan(combine, (a, u), axis=1)
    return h.astype(x.dtype)


def compress_blocks(blocks, pe, w1, w2):
    z = (blocks + pe).reshape(blocks.shape[:-2] + (CMP_LEN * HEAD_DIM,))
    return jax.nn.silu(z @ w1) @ w2


def nsa(q, k_c, v_c, k_s, v_s, k_w, v_w, gates, q_g, k_g,
        pe_k, w1_k, w2_k, pe_v, w1_v, w2_v):
    B, S, _ = q.shape
    pos = jnp.arange(S, dtype=jnp.float32)
    scale = 1.0 / math.sqrt(HEAD_DIM)
    n_cmp = (S - CMP_LEN) // CMP_STRIDE + 1
    n_sel = S // SEL_LEN
    top_n = min(SEL_TOPN, n_sel)
    n_qb = S // Q_BLOCK

    def kv_heads(t):
        return t.reshape(B, S, N_KV, HEAD_DIM).transpose(0, 2, 1, 3)

    qh = q.reshape(B, S, N_KV, HPG, HEAD_DIM).transpose(0, 2, 3, 1, 4)
    qh = rope_partial(rms_norm(qh, q_g), pos)

    blk_idx = np.arange(n_cmp)[:, None] * CMP_STRIDE + np.arange(CMP_LEN)[None, :]
    kc = compress_blocks(kv_heads(k_c)[:, :, blk_idx], pe_k, w1_k, w2_k)
    vc = compress_blocks(kv_heads(v_c)[:, :, blk_idx], pe_v, w1_v, w2_v)
    cmp_end = jnp.arange(n_cmp) * CMP_STRIDE + (CMP_LEN - 1)
    kc = rope_partial(rms_norm(kc, k_g[0]), cmp_end.astype(jnp.float32))

    cs = np.arange(n_cmp)[:, None] * CMP_STRIDE
    ss = np.arange(n_sel)[None, :] * SEL_LEN
    ov = np.maximum(0, np.minimum(cs + CMP_LEN, ss + SEL_LEN) - np.maximum(cs, ss))
    overlap = jnp.asarray((ov / CMP_LEN).astype(np.float32))

    ks = rope_partial(rms_norm(kv_heads(k_s), k_g[1]), pos)
    ks_blk = ks.reshape(B, N_KV, n_sel, SEL_LEN * HEAD_DIM)
    vs_blk = kv_heads(v_s).reshape(B, N_KV, n_sel, SEL_LEN * HEAD_DIM)

    kw = rope_partial(rms_norm(kv_heads(k_w), k_g[2]), pos)
    kw_pad = jnp.pad(kw, ((0, 0), (0, 0), (WINDOW, 0), (0, 0)))
    vw_pad = jnp.pad(kv_heads(v_w), ((0, 0), (0, 0), (WINDOW, 0), (0, 0)))

    g = jax.nn.sigmoid(gates).reshape(B, S, N_KV, HPG, N_BRANCH).transpose(0, 2, 3, 1, 4)
    j_sel = jnp.arange(n_sel)

    def q_block(qi):
        q0 = qi * Q_BLOCK
        t = q0 + jnp.arange(Q_BLOCK)
        qb = lax.dynamic_slice_in_dim(qh, q0, Q_BLOCK, axis=3)
        gb = lax.dynamic_slice_in_dim(g, q0, Q_BLOCK, axis=3)
        s_c = jnp.einsum('bghqd,bgkd->bghqk', qb, kc) * scale
        p_c = masked_softmax(s_c, cmp_end[None, :] <= t[:, None])
        o_c = jnp.einsum('bghqk,bgkd->bghqd', p_c.astype(vc.dtype), vc)
        imp = jnp.einsum('bghqk,kj->bgqj', p_c, overlap)
        bt = (t // SEL_LEN)[:, None]
        force = (j_sel[None, :] == 0) | (j_sel[None, :] == bt) | (j_sel[None, :] == bt - 1)
        imp = jnp.where(force, BIG, jnp.where(j_sel[None, :] <= bt, imp, NEG))
        _, idx = lax.top_k(imp, top_n)
        flat = idx.reshape(B, N_KV, Q_BLOCK * top_n)[..., None]
        kg = jnp.take_along_axis(ks_blk, flat, axis=2).reshape(
            B, N_KV, Q_BLOCK, top_n * SEL_LEN, HEAD_DIM)
        vg = jnp.take_along_axis(vs_blk, flat, axis=2).reshape(
            B, N_KV, Q_BLOCK, top_n * SEL_LEN, HEAD_DIM)
        kpos = (idx[..., None] * SEL_LEN + jnp.arange(SEL_LEN)).reshape(
            B, N_KV, Q_BLOCK, top_n * SEL_LEN)
        m_s = (kpos <= t[None, None, :, None])[:, :, None]
        s_s = jnp.einsum('bghqd,bgqkd->bghqk', qb, kg) * scale
        p_s = masked_softmax(s_s, m_s)
        o_s = jnp.einsum('bghqk,bgqkd->bghqd', p_s.astype(vg.dtype), vg)
        kwb = lax.dynamic_slice_in_dim(kw_pad, q0, WINDOW + Q_BLOCK, axis=2)
        vwb = lax.dynamic_slice_in_dim(vw_pad, q0, WINDOW + Q_BLOCK, axis=2)
        kp = (q0 - WINDOW + jnp.arange(WINDOW + Q_BLOCK))[None, :]
        m_w = (kp <= t[:, None]) & (kp > t[:, None] - WINDOW) & (kp >= 0)
        s_w = jnp.einsum('bghqd,bgkd->bghqk', qb, kwb) * scale
        p_w = masked_softmax(s_w, m_w)
        o_w = jnp.einsum('bghqk,bgkd->bghqd', p_w.astype(vwb.dtype), vwb)
        return gb[..., 0:1] * o_c + gb[..., 1:2] * o_s + gb[..., 2:3] * o_w

    outs = lax.map(q_block, jnp.arange(n_qb))
    return outs.transpose(1, 0, 4, 2, 3, 5).reshape(B, S, NSA_WIDTH)


def hybrid_layer(x, norm_g, w_in, conv_w, conv_b, rg_wr, rg_br, rg_wi, rg_bi, rg_lam,
                 q_g, k_g, pe_k, w1_k, w2_k, pe_v, w1_v, w2_v, w_out):
    h = rms_norm(x, norm_g)
    z = h @ w_in
    sizes = [RG_WIDTH, RG_WIDTH, NSA_WIDTH] + [KV_WIDTH] * 6 + [NSA_WIDTH]
    cuts = list(np.cumsum(sizes))
    (rg_x, rg_gate, q, k_c, v_c, k_s, v_s, k_w, v_w, nsa_gate, br_gate) = jnp.split(
        z, cuts, axis=-1)
    y_a = rg_lru(causal_depthwise_conv(rg_x, conv_w, conv_b), rg_wr, rg_br, rg_wi, rg_bi, rg_lam)
    y_a = y_a * jax.nn.silu(rg_gate)
    y_b = nsa(q, k_c, v_c, k_s, v_s, k_w, v_w, br_gate, q_g, k_g,
              pe_k, w1_k, w2_k, pe_v, w1_v, w2_v)
    y_b = y_b * jax.nn.silu(nsa_gate)
    y = jnp.concatenate([y_a, y_b], axis=-1)
    return x + y @ w_out


def setup_inputs(seed: int = 0) -> dict:
    key = jax.random.key(seed)
    ks = jax.random.split(key, 20)
    f32 = jnp.float32
    n = jax.random.normal
    a0 = jax.random.uniform(ks[10], (DEPTH, RG_WIDTH), f32, A_MIN, A_MAX)
    s0 = a0 ** (1.0 / RG_C)
    return {
        "x": n(ks[0], (BATCH, SEQ, D_MODEL), f32),
        "norm_g": 1.0 + 0.02 * n(ks[1], (DEPTH, D_MODEL), f32),
        "w_in": n(ks[2], (DEPTH, D_MODEL, N_IN), f32) * D_MODEL ** -0.5,
        "conv_w": n(ks[3], (DEPTH, CONV_W, RG_WIDTH), f32) * CONV_W ** -0.5,
        "conv_b": 0.01 * n(ks[4], (DEPTH, RG_WIDTH), f32),
        "rg_wr": n(ks[5], (DEPTH, RG_BLOCKS, RG_BLOCK, RG_BLOCK), f32) * RG_BLOCK ** -0.5,
        "rg_br": 0.01 * n(ks[6], (DEPTH, RG_WIDTH), f32),
        "rg_wi": n(ks[7], (DEPTH, RG_BLOCKS, RG_BLOCK, RG_BLOCK), f32) * RG_BLOCK ** -0.5,
        "rg_bi": 0.01 * n(ks[8], (DEPTH, RG_WIDTH), f32),
        "rg_lambda": jnp.log(s0) - jnp.log1p(-s0),
        "q_norm_g": 1.0 + 0.02 * n(ks[9], (DEPTH, HEAD_DIM), f32),
        "k_norm_g": 1.0 + 0.02 * n(ks[11], (DEPTH, N_BRANCH, HEAD_DIM), f32),
        "cmp_pe_k": 0.02 * n(ks[12], (DEPTH, CMP_LEN, HEAD_DIM), f32),
        "cmp_w1_k": n(ks[13], (DEPTH, CMP_LEN * HEAD_DIM, CMP_HIDDEN), f32) * (CMP_LEN * HEAD_DIM) ** -0.5,
        "cmp_w2_k": n(ks[14], (DEPTH, CMP_HIDDEN, HEAD_DIM), f32) * CMP_HIDDEN ** -0.5,
        "cmp_pe_v": 0.02 * n(ks[15], (DEPTH, CMP_LEN, HEAD_DIM), f32),
        "cmp_w1_v": n(ks[16], (DEPTH, CMP_LEN * HEAD_DIM, CMP_HIDDEN), f32) * (CMP_LEN * HEAD_DIM) ** -0.5,
        "cmp_w2_v": n(ks[17], (DEPTH, CMP_HIDDEN, HEAD_DIM), f32) * CMP_HIDDEN ** -0.5,
        "w_out": n(ks[18], (DEPTH, D_MIX, D_MODEL), f32) * D_MIX ** -0.5,
    }


def reference(x, norm_g, w_in, conv_w, conv_b, rg_wr, rg_br, rg_wi, rg_bi, rg_lambda,
              q_norm_g, k_norm_g, cmp_pe_k, cmp_w1_k, cmp_w2_k, cmp_pe_v, cmp_w1_v,
              cmp_w2_v, w_out):
    for l in range(DEPTH):
        x = hybrid_layer(x, norm_g[l], w_in[l], conv_w[l], conv_b[l], rg_wr[l], rg_br[l],
                         rg_wi[l], rg_bi[l], rg_lambda[l], q_norm_g[l], k_norm_g[l],
                         cmp_pe_k[l], cmp_w1_k[l], cmp_w2_k[l], cmp_pe_v[l], cmp_w1_v[l],
                         cmp_w2_v[l], w_out[l])
    return x
```

```python
import functools
import math

import numpy as np
import jax
import jax.numpy as jnp
from jax import lax
from jax.experimental import pallas as pl
from jax.experimental.pallas import tpu as pltpu

D_MODEL = 1024
RG_WIDTH = 512
RG_BLOCKS = 8
RG_BLOCK = 64
CONV_W = 4
RG_C = 8.0
NSA_WIDTH = 512
HEAD_DIM = 64
N_KV = 2
HPG = 4
KV_WIDTH = N_KV * HEAD_DIM
ROPE_DIM = 16
ROPE_THETA = 500000.0
CMP_LEN = 32
CMP_STRIDE = 16
CMP_HIDDEN = 128
SEL_LEN = 64
SEL_TOPN = 16
WINDOW = 512
N_BRANCH = 3
Q_BLOCK = 128
EPS = 1e-6
NEG = -1e30
N_IN = 2 * RG_WIDTH + 2 * NSA_WIDTH + 6 * KV_WIDTH + N_BRANCH * 8
LANE = 128
N_PAD = ((N_IN + LANE - 1) // LANE) * LANE

COL_Q = 2 * RG_WIDTH
COL_KC = COL_Q + NSA_WIDTH
COL_VC = COL_KC + KV_WIDTH
COL_KS = COL_VC + KV_WIDTH
COL_VS = COL_KS + KV_WIDTH
COL_KW = COL_VS + KV_WIDTH
COL_VW = COL_KW + KV_WIDTH
COL_NG = COL_VW + KV_WIDTH
COL_BG = COL_NG + NSA_WIDTH

VMEM_LIMIT = 56 * 1024 * 1024

ROW_TILE = 512
SEL_TILE = 256
WIN_TILE = 128


def _cparams(sem):
    return pltpu.CompilerParams(dimension_semantics=sem, vmem_limit_bytes=VMEM_LIMIT)


def _in_proj_body(x_ref, g_ref, w_ref, z_ref):
    x = x_ref[...]
    ms = jnp.mean(x * x, axis=-1, keepdims=True)
    h = (x * lax.rsqrt(ms + EPS)) * g_ref[...]
    z_ref[...] = jnp.dot(h.astype(jnp.bfloat16), w_ref[...], preferred_element_type=jnp.float32)


def _in_proj(x2, g, w):
    rows = x2.shape[0]
    return pl.pallas_call(
        _in_proj_body,
        out_shape=jax.ShapeDtypeStruct((rows, N_PAD), jnp.float32),
        grid=(rows // ROW_TILE,),
        in_specs=[
            pl.BlockSpec((ROW_TILE, D_MODEL), lambda i: (i, 0)),
            pl.BlockSpec((1, D_MODEL), lambda i: (0, 0)),
            pl.BlockSpec((D_MODEL, N_PAD), lambda i: (0, 0)),
        ],
        out_specs=pl.BlockSpec((ROW_TILE, N_PAD), lambda i: (i, 0)),
        compiler_params=_cparams(("arbitrary",)),
        name="in_proj",
    )(x2, g, w)


def _head_mean_sq(x, ind):
    sq = x * x
    hi = sq.astype(jnp.bfloat16)
    lo = (sq - hi.astype(jnp.float32)).astype(jnp.bfloat16)
    tot = (jnp.dot(hi, ind, preferred_element_type=jnp.float32)
           + jnp.dot(lo, ind, preferred_element_type=jnp.float32))
    return tot * (1.0 / HEAD_DIM)


def _norm_rope(x, gain, ind, ra, rm, rp):
    width = x.shape[-1]
    y = (x * lax.rsqrt(_head_mean_sq(x, ind) + EPS)) * gain
    up = pltpu.roll(y, width - ROPE_DIM // 2, axis=1)
    dn = pltpu.roll(y, ROPE_DIM // 2, axis=1)
    return y * ra + up * rm + dn * rp


def _prep_body(q_ref, ks_ref, vs_ref, kw_ref, vw_ref, bg_ref, ra_ref, rm_ref, rp_ref,
               qg_ref, kgs_ref, kgw_ref, ind_ref,
               qt_ref, kso_ref, vst_ref, kwo_ref, vwt_ref, gt_ref):
    ind = ind_ref[...]
    ra, rm, rp = ra_ref[...], rm_ref[...], rp_ref[...]
    scale = 1.0 / math.sqrt(HEAD_DIM)
    for c in range(NSA_WIDTH // LANE):
        sl = slice(c * LANE, (c + 1) * LANE)
        qn = _norm_rope(q_ref[:, sl], qg_ref[...], ind, ra, rm, rp) * scale
        qt_ref[0, sl, :] = qn.T.astype(jnp.bfloat16)
    kso_ref[0] = _norm_rope(ks_ref[...], kgs_ref[...], ind, ra, rm, rp).astype(jnp.bfloat16)
    kwo_ref[0] = _norm_rope(kw_ref[...], kgw_ref[...], ind, ra, rm, rp).astype(jnp.bfloat16)
    vst_ref[0] = vs_ref[...].T.astype(jnp.bfloat16)
    vwt_ref[0] = vw_ref[...].T.astype(jnp.bfloat16)
    gt_ref[0] = jax.nn.sigmoid(bg_ref[...]).T


def _prep(z, B, S, ra, rm, rp, qg, kgs, kgw, ind):
    nt = S // ROW_TILE

    def zspec(col, width):
        return pl.BlockSpec((ROW_TILE, width), lambda b, i: (b * nt + i, col // width))

    tab = pl.BlockSpec((ROW_TILE, LANE), lambda b, i: (i, 0))
    par = pl.BlockSpec((1, LANE), lambda b, i: (0, 0))
    bf = jnp.bfloat16
    return pl.pallas_call(
        _prep_body,
        out_shape=(
            jax.ShapeDtypeStruct((B, NSA_WIDTH, S), bf),
            jax.ShapeDtypeStruct((B, S, KV_WIDTH), bf),
            jax.ShapeDtypeStruct((B, KV_WIDTH, S), bf),
            jax.ShapeDtypeStruct((B, S, KV_WIDTH), bf),
            jax.ShapeDtypeStruct((B, KV_WIDTH, S), bf),
            jax.ShapeDtypeStruct((B, LANE, S), jnp.float32),
        ),
        grid=(B, nt),
        in_specs=[
            zspec(COL_Q, NSA_WIDTH), zspec(COL_KS, KV_WIDTH), zspec(COL_VS, KV_WIDTH),
            zspec(COL_KW, KV_WIDTH), zspec(COL_VW, KV_WIDTH), zspec(COL_BG, LANE),
            tab, tab, tab, par, par, par,
            pl.BlockSpec((LANE, LANE), lambda b, i: (0, 0)),
        ],
        out_specs=(
            pl.BlockSpec((1, NSA_WIDTH, ROW_TILE), lambda b, i: (b, 0, i)),
            pl.BlockSpec((1, ROW_TILE, KV_WIDTH), lambda b, i: (b, i, 0)),
            pl.BlockSpec((1, KV_WIDTH, ROW_TILE), lambda b, i: (b, 0, i)),
            pl.BlockSpec((1, ROW_TILE, KV_WIDTH), lambda b, i: (b, i, 0)),
            pl.BlockSpec((1, KV_WIDTH, ROW_TILE), lambda b, i: (b, 0, i)),
            pl.BlockSpec((1, LANE, ROW_TILE), lambda b, i: (b, 0, i)),
        ),
        compiler_params=_cparams(("arbitrary", "arbitrary")),
        name="prep",
    )(z, z, z, z, z, z, ra, rm, rp, qg, kgs, kgw, ind)


RG_TILE = 512
SUB = 8


def _rglru_body(x_ref, gate_ref, cw_ref, cb_ref, wr_ref, br_ref, wi_ref, bi_ref, lam_ref,
                y_ref, tail_ref, carry_ref, a_ref, b_ref):
    @pl.when(pl.program_id(1) == 0)
    def _():
        tail_ref[...] = jnp.zeros_like(tail_ref)
        carry_ref[...] = jnp.zeros_like(carry_ref)

    x = x_ref[...]
    tail = tail_ref[...]
    row8 = lax.broadcasted_iota(jnp.int32, (SUB, RG_WIDTH), 0)
    conv = x * cw_ref[CONV_W - 1:CONV_W, :] + cb_ref[...]
    for s in range(1, CONV_W):
        xs = pltpu.roll(x, s, axis=0)
        first = jnp.where(row8 < s, pltpu.roll(tail, s, axis=0), xs[0:SUB])
        xs = jnp.concatenate([first, xs[SUB:]], axis=0)
        conv = conv + xs * cw_ref[CONV_W - 1 - s:CONV_W - s, :]
    tail_ref[...] = x[RG_TILE - SUB:]

    cb16 = conv.astype(jnp.bfloat16)
    r = jax.nn.sigmoid(jnp.dot(cb16, wr_ref[...], preferred_element_type=jnp.float32) + br_ref[...])
    ig = jax.nn.sigmoid(jnp.dot(cb16, wi_ref[...], preferred_element_type=jnp.float32) + bi_ref[...])
    log_a = (-RG_C * r) * jax.nn.softplus(-lam_ref[...])
    a = jnp.exp(log_a)
    u = jnp.sqrt(-jnp.tanh(log_a) * (jnp.exp(2.0 * log_a) + 1.0)) * (ig * conv)

    row = lax.broadcasted_iota(jnp.int32, (RG_TILE, RG_WIDTH), 0) % SUB
    for d in (1, 2, 4):
        keep = row >= d
        a_s = jnp.where(keep, pltpu.roll(a, d, axis=0), 1.0)
        u_s = jnp.where(keep, pltpu.roll(u, d, axis=0), 0.0)
        u = a * u_s + u
        a = a * a_s
    a_ref[...] = a
    b_ref[...] = u

    def step(k, h):
        off = pl.multiple_of(k * SUB, SUB)
        hk = a_ref[pl.ds(off, SUB), :] * h + b_ref[pl.ds(off, SUB), :]
        b_ref[pl.ds(off, SUB), :] = hk
        return jnp.broadcast_to(hk[SUB - 1:SUB, :], (SUB, RG_WIDTH))

    carry_ref[...] = lax.fori_loop(0, RG_TILE // SUB, step, carry_ref[...])
    gate = gate_ref[...]
    y_ref[...] = (b_ref[...] * (gate * jax.nn.sigmoid(gate))).astype(y_ref.dtype)


def _rglru(z, B, S, cw, cb, wr, br, wi, bi, lam):
    nt = S // RG_TILE
    vec = pl.BlockSpec((1, RG_WIDTH), lambda b, i: (0, 0))
    mat = pl.BlockSpec((RG_WIDTH, RG_WIDTH), lambda b, i: (0, 0))
    return pl.pallas_call(
        _rglru_body,
        out_shape=jax.ShapeDtypeStruct((B * S, RG_WIDTH), jnp.bfloat16),
        grid=(B, nt),
        in_specs=[
            pl.BlockSpec((RG_TILE, RG_WIDTH), lambda b, i: (b * nt + i, 0)),
            pl.BlockSpec((RG_TILE, RG_WIDTH), lambda b, i: (b * nt + i, 1)),
            pl.BlockSpec((CONV_W, RG_WIDTH), lambda b, i: (0, 0)),
            vec, mat, vec, mat, vec, vec,
        ],
        out_specs=pl.BlockSpec((RG_TILE, RG_WIDTH), lambda b, i: (b * nt + i, 0)),
        scratch_shapes=[
            pltpu.VMEM((SUB, RG_WIDTH), jnp.float32),
            pltpu.VMEM((SUB, RG_WIDTH), jnp.float32),
            pltpu.VMEM((RG_TILE, RG_WIDTH), jnp.float32),
            pltpu.VMEM((RG_TILE, RG_WIDTH), jnp.float32),
        ],
        compiler_params=_cparams(("arbitrary", "arbitrary")),
        name="rglru",
    )(z, z, cw, cb, wr, br, wi, bi, lam)


N_CHUNK_TOK = CMP_STRIDE


def _compress_one(src_ref, wa_ref, wb_ref, pea_ref, peb_ref, w2_ref, n_chunk):
    ha = jnp.zeros((n_chunk, 2 * CMP_HIDDEN), jnp.float32)
    hb = jnp.zeros((n_chunk, 2 * CMP_HIDDEN), jnp.float32)
    for j in range(N_CHUNK_TOK):
        xj = src_ref[pl.ds(j, n_chunk, stride=N_CHUNK_TOK), :]
        ha = ha + jnp.dot((xj + pea_ref[j]).astype(jnp.bfloat16), wa_ref[j],
                          preferred_element_type=jnp.float32)
        hb = hb + jnp.dot((xj + peb_ref[j]).astype(jnp.bfloat16), wb_ref[j],
                          preferred_element_type=jnp.float32)
    hid = ha + pltpu.roll(hb, n_chunk - 1, axis=0)
    act = hid * jax.nn.sigmoid(hid)
    return jnp.dot(act.astype(jnp.bfloat16), w2_ref[...], preferred_element_type=jnp.float32)


def _compress_body(kc_ref, vc_ref, wak_ref, wbk_ref, peak_ref, pebk_ref, w2k_ref,
                   wav_ref, wbv_ref, peav_ref, pebv_ref, w2v_ref,
                   kg_ref, ind_ref, ra_ref, rm_ref, rp_ref, ko_ref, vto_ref, *, n_chunk):
    kc = _compress_one(kc_ref, wak_ref, wbk_ref, peak_ref, pebk_ref, w2k_ref, n_chunk)
    ko_ref[0] = _norm_rope(kc, kg_ref[...], ind_ref[...], ra_ref[...], rm_ref[...],
                           rp_ref[...]).astype(jnp.bfloat16)
    vc = _compress_one(vc_ref, wav_ref, wbv_ref, peav_ref, pebv_ref, w2v_ref, n_chunk)
    vto_ref[0] = vc.T.astype(jnp.bfloat16)


def _compress(z, B, S, wk, wv, kg, ind, ra, rm, rp):
    n_chunk = S // N_CHUNK_TOK

    def full(a):
        nd = a.ndim
        return pl.BlockSpec(a.shape, lambda b: (0,) * nd)

    consts = list(wk) + list(wv) + [kg, ind, ra, rm, rp]
    return pl.pallas_call(
        functools.partial(_compress_body, n_chunk=n_chunk),
        out_shape=(
            jax.ShapeDtypeStruct((B, n_chunk, KV_WIDTH), jnp.bfloat16),
            jax.ShapeDtypeStruct((B, KV_WIDTH, n_chunk), jnp.bfloat16),
        ),
        grid=(B,),
        in_specs=[
            pl.BlockSpec((S, KV_WIDTH), lambda b: (b, COL_KC // KV_WIDTH)),
            pl.BlockSpec((S, KV_WIDTH), lambda b: (b, COL_VC // KV_WIDTH)),
        ] + [full(a) for a in consts],
        out_specs=(
            pl.BlockSpec((1, n_chunk, KV_WIDTH), lambda b: (b, 0, 0)),
            pl.BlockSpec((1, KV_WIDTH, n_chunk), lambda b: (b, 0, 0)),
        ),
        compiler_params=_cparams(("arbitrary",)),
        name="compress",
    )(z, z, *consts)


N_COL = HPG * Q_BLOCK


def _softmax_step(s, valid, vt, state):
    m, l, acc = state
    m_new = jnp.maximum(m, jnp.max(s, axis=0, keepdims=True))
    alpha = jnp.exp(m - m_new)
    p = jnp.exp(s - m_new)
    if valid is not None:
        p = jnp.where(valid, p, 0.0)
    l = alpha * l + jnp.sum(p, axis=0, keepdims=True)
    acc = alpha * acc + jnp.dot(vt, p.astype(jnp.bfloat16), preferred_element_type=jnp.float32)
    return m_new, l, acc


def _finish(state):
    _, l, acc = state
    return acc * jnp.where(l > 0.0, 1.0 / l, 0.0)


def _init_state():
    return (jnp.full((1, N_COL), NEG, jnp.float32), jnp.zeros((1, N_COL), jnp.float32),
            jnp.zeros((HEAD_DIM, N_COL), jnp.float32))


def _nsa_body(qt_ref, kc_ref, vct_ref, ks_ref, vst_ref, kw_ref, vwt_ref, gt_ref, ng_ref,
              y_ref, ps_ref, bias_ref, *, n_cmp_pad):
    g = pl.program_id(1)
    qi = pl.program_id(2)
    q0 = qi * Q_BLOCK
    goff = pl.multiple_of(g * HEAD_DIM, HEAD_DIM)

    qt = qt_ref[0]
    qcat = jnp.concatenate([qt[h * HEAD_DIM:(h + 1) * HEAD_DIM, :] for h in range(HPG)], axis=1)
    qcat2 = jnp.concatenate([qcat] * N_KV, axis=0)
    rowgrp = lax.broadcasted_iota(jnp.int32, (KV_WIDTH, N_COL), 0) // HEAD_DIM
    qz = jnp.where(rowgrp == g, qcat2, jnp.zeros_like(qcat2))

    tcol = q0 + lax.broadcasted_iota(jnp.int32, (1, N_COL), 1) % Q_BLOCK

    sc = jnp.dot(kc_ref[0], qz, preferred_element_type=jnp.float32)
    cend = lax.broadcasted_iota(jnp.int32, (n_cmp_pad, 1), 0) * CMP_STRIDE + (CMP_LEN - 1)
    cvalid = cend <= tcol
    scm = jnp.where(cvalid, sc, NEG)
    mc = jnp.max(scm, axis=0, keepdims=True)
    pc = jnp.where(cvalid, jnp.exp(scm - mc), 0.0)
    lc = jnp.sum(pc, axis=0, keepdims=True)
    pc = pc * jnp.where(lc > 0.0, 1.0 / lc, 0.0)
    o_c = jnp.dot(vct_ref[0, pl.ds(goff, HEAD_DIM), :], pc.astype(jnp.bfloat16),
                  preferred_element_type=jnp.float32)

    psum = pc[:, 0:Q_BLOCK]
    for h in range(1, HPG):
        psum = psum + pc[:, h * Q_BLOCK:(h + 1) * Q_BLOCK]
    ps_ref[0:SUB, :] = jnp.zeros((SUB, Q_BLOCK), jnp.float32)
    ps_ref[SUB:, :] = psum
    n_sel = n_cmp_pad // 4
    per = SEL_LEN // CMP_STRIDE

    def strided(k):
        return ps_ref[pl.ds(SUB + k, n_sel, stride=per), :]

    imp = strided(0) + strided(1) + strided(2) + 0.5 * strided(3) + 0.5 * strided(-1)

    tq = q0 + lax.broadcasted_iota(jnp.int32, (n_sel, Q_BLOCK), 1)
    bt = tq // SEL_LEN
    rowj = lax.broadcasted_iota(jnp.int32, (n_sel, Q_BLOCK), 0)
    forced = (rowj == 0) | (rowj == bt) | (rowj == bt - 1)
    valid = rowj <= bt
    cand = jnp.where(valid & jnp.logical_not(forced), imp, -1.0)
    bias = jnp.where(forced & valid, 0.0, NEG)
    rowf = rowj.astype(jnp.float32)

    def pick(_, carry):
        cand, bias = carry
        mx = jnp.max(cand, axis=0, keepdims=True)
        first = jnp.min(jnp.where(cand == mx, rowf, float(n_sel)), axis=0, keepdims=True)
        hit = (rowf == first) & (mx >= 0.0)
        return jnp.where(hit, -1.0, cand), jnp.where(hit, 0.0, bias)

    _, bias = lax.fori_loop(0, SEL_TOPN - 3, pick, (cand, bias))
    bias_ref[...] = bias

    blocks_per_tile = SEL_TILE // SEL_LEN

    def sel_scores(kt):
        koff = pl.multiple_of(kt * SEL_TILE, SEL_TILE)
        s = jnp.dot(ks_ref[0, pl.ds(koff, SEL_TILE), :], qz, preferred_element_type=jnp.float32)
        rows = []
        for blk in range(blocks_per_tile):
            brow = bias_ref[pl.ds(kt * blocks_per_tile + blk, 1), :]
            brow = jnp.concatenate([brow] * HPG, axis=1)
            rows.append(s[blk * SEL_LEN:(blk + 1) * SEL_LEN, :] + brow)
        vt = vst_ref[0, pl.ds(goff, HEAD_DIM), pl.ds(koff, SEL_TILE)]
        return jnp.concatenate(rows, axis=0), vt, koff

    def sel_step(kt, state):
        s, vt, _ = sel_scores(kt)
        return _softmax_step(s, None, vt, state)

    n_full = (q0 + Q_BLOCK - 1) // SEL_TILE
    state = lax.fori_loop(0, n_full, sel_step, _init_state())
    s, vt, koff = sel_scores(n_full)
    kpos = koff + lax.broadcasted_iota(jnp.int32, (SEL_TILE, 1), 0)
    causal = kpos <= tcol
    o_s = _finish(_softmax_step(jnp.where(causal, s, NEG), causal, vt, state))

    def win_step(w, state):
        start = q0 - WINDOW + w * WIN_TILE
        koff = pl.multiple_of(start, WIN_TILE)
        s = jnp.dot(kw_ref[0, pl.ds(koff, WIN_TILE), :], qz, preferred_element_type=jnp.float32)
        kpos = koff + lax.broadcasted_iota(jnp.int32, (WIN_TILE, 1), 0)
        ok = (kpos <= tcol) & (kpos > tcol - WINDOW)
        vt = vwt_ref[0, pl.ds(goff, HEAD_DIM), pl.ds(koff, WIN_TILE)]
        return _softmax_step(jnp.where(ok, s, NEG), ok, vt, state)

    n_win = WINDOW // WIN_TILE + 1
    w0 = jnp.maximum(0, n_win - 1 - qi)
    o_w = _finish(lax.fori_loop(w0, n_win, win_step, _init_state()))

    def gate_row(branch):
        return jnp.concatenate(
            [gt_ref[0, pl.ds(g * (HPG * N_BRANCH) + h * N_BRANCH + branch, 1), :] for h in range(HPG)],
            axis=1)

    yt = gate_row(0) * o_c + gate_row(1) * o_s + gate_row(2) * o_w
    halves = []
    for pair in range(HPG // 2):
        two = jnp.concatenate([yt[:, (2 * pair) * Q_BLOCK:(2 * pair + 1) * Q_BLOCK],
                               yt[:, (2 * pair + 1) * Q_BLOCK:(2 * pair + 2) * Q_BLOCK]], axis=0)
        halves.append(two.T)
    y = jnp.concatenate(halves, axis=1)
    ng = ng_ref[...]
    y_ref[...] = (y * (ng * jax.nn.sigmoid(ng))).astype(y_ref.dtype)


def _nsa(z, qt, kc, vct, ks, vst, kw, vwt, gt, B, S):
    nq = S // Q_BLOCK
    n_cmp_pad = S // CMP_STRIDE
    gw = HPG * HEAD_DIM
    return pl.pallas_call(
        functools.partial(_nsa_body, n_cmp_pad=n_cmp_pad),
        out_shape=jax.ShapeDtypeStruct((B * S, NSA_WIDTH), jnp.bfloat16),
        grid=(B, N_KV, nq),
        in_specs=[
            pl.BlockSpec((1, gw, Q_BLOCK), lambda b, g, i: (b, g, i)),
            pl.BlockSpec((1, n_cmp_pad, KV_WIDTH), lambda b, g, i: (b, 0, 0)),
            pl.BlockSpec((1, KV_WIDTH, n_cmp_pad), lambda b, g, i: (b, 0, 0)),
            pl.BlockSpec((1, S, KV_WIDTH), lambda b, g, i: (b, 0, 0)),
            pl.BlockSpec((1, KV_WIDTH, S), lambda b, g, i: (b, 0, 0)),
            pl.BlockSpec((1, S, KV_WIDTH), lambda b, g, i: (b, 0, 0)),
            pl.BlockSpec((1, KV_WIDTH, S), lambda b, g, i: (b, 0, 0)),
            pl.BlockSpec((1, LANE, Q_BLOCK), lambda b, g, i: (b, 0, i)),
            pl.BlockSpec((Q_BLOCK, gw), lambda b, g, i: (b * nq + i, COL_NG // gw + g)),
        ],
        out_specs=pl.BlockSpec((Q_BLOCK, gw), lambda b, g, i: (b * nq + i, g)),
        scratch_shapes=[
            pltpu.VMEM((SUB + n_cmp_pad, Q_BLOCK), jnp.float32),
            pltpu.VMEM((n_cmp_pad // 4, Q_BLOCK), jnp.float32),
        ],
        compiler_params=_cparams(("arbitrary", "arbitrary", "arbitrary")),
        name="nsa",
    )(qt, kc, vct, ks, vst, kw, vwt, gt, z)


def _out_proj_body(x_ref, ya_ref, yb_ref, wa_ref, wb_ref, o_ref):
    o_ref[...] = (x_ref[...]
                  + jnp.dot(ya_ref[...], wa_ref[...], preferred_element_type=jnp.float32)
                  + jnp.dot(yb_ref[...], wb_ref[...], preferred_element_type=jnp.float32))


def _out_proj(x2, ya, yb, wa, wb):
    rows = x2.shape[0]
    return pl.pallas_call(
        _out_proj_body,
        out_shape=jax.ShapeDtypeStruct((rows, D_MODEL), jnp.float32),
        grid=(rows // ROW_TILE,),
        in_specs=[
            pl.BlockSpec((ROW_TILE, D_MODEL), lambda i: (i, 0)),
            pl.BlockSpec((ROW_TILE, RG_WIDTH), lambda i: (i, 0)),
            pl.BlockSpec((ROW_TILE, NSA_WIDTH), lambda i: (i, 0)),
            pl.BlockSpec((RG_WIDTH, D_MODEL), lambda i: (0, 0)),
            pl.BlockSpec((NSA_WIDTH, D_MODEL), lambda i: (0, 0)),
        ],
        out_specs=pl.BlockSpec((ROW_TILE, D_MODEL), lambda i: (i, 0)),
        compiler_params=_cparams(("arbitrary",)),
        name="out_proj",
    )(x2, ya, yb, wa, wb)


def _rope_tables(pos):
    half = ROPE_DIM // 2
    inv = ROPE_THETA ** (-jnp.arange(half, dtype=jnp.float32) / half)
    ang = pos.astype(jnp.float32)[:, None] * inv[None, :]
    cos, sin = jnp.cos(ang), jnp.sin(ang)
    n = pos.shape[0]
    rest = HEAD_DIM - ROPE_DIM
    ra = jnp.concatenate([cos, cos, jnp.ones((n, rest), jnp.float32)], axis=1)
    rm = jnp.concatenate([-sin, jnp.zeros((n, half + rest), jnp.float32)], axis=1)
    rp = jnp.concatenate([jnp.zeros((n, half), jnp.float32), sin, jnp.zeros((n, rest), jnp.float32)], axis=1)
    return tuple(jnp.tile(t, (1, LANE // HEAD_DIM)) for t in (ra, rm, rp))


def _block_diag(blocks):
    n, r, c = blocks.shape
    eye = jnp.eye(n, dtype=blocks.dtype)
    return (eye[:, None, :, None] * blocks[:, :, None, :]).reshape(n * r, n * c)


def _compress_params(pe, w1, w2):
    bf = jnp.bfloat16
    w1t = w1.reshape(2, N_CHUNK_TOK, HEAD_DIM, CMP_HIDDEN)
    dup = lambda w: jax.vmap(lambda m: _block_diag(jnp.stack([m] * N_KV)))(w)
    wa, wb = dup(w1t[0]).astype(bf), dup(w1t[1]).astype(bf)
    pet = jnp.tile(pe.reshape(2, N_CHUNK_TOK, 1, HEAD_DIM), (1, 1, 1, N_KV))
    w2d = _block_diag(jnp.stack([w2] * N_KV)).astype(bf)
    return wa, wb, pet[0], pet[1], w2d


def _layer(x2, B, S, norm_g, w_in, conv_w, conv_b, rg_wr, rg_br, rg_wi, rg_bi, rg_lam,
           q_g, k_g, pe_k, w1_k, w2_k, pe_v, w1_v, w2_v, w_out, tabs_tok, tabs_cmp, ind):
    bf = jnp.bfloat16
    w_in_p = jnp.pad(w_in, ((0, 0), (0, N_PAD - N_IN))).astype(bf)
    z = _in_proj(x2, norm_g[None, :], w_in_p)

    y_a = _rglru(z, B, S, conv_w, conv_b[None, :], _block_diag(rg_wr).astype(bf), rg_br[None, :],
                 _block_diag(rg_wi).astype(bf), rg_bi[None, :], rg_lam[None, :])

    tile2 = lambda v: jnp.tile(v, LANE // HEAD_DIM)[None, :]
    qt, ks, vst, kw, vwt, gt = _prep(z, B, S, *tabs_tok, tile2(q_g), tile2(k_g[1]), tile2(k_g[2]), ind)
    kc, vct = _compress(z, B, S, _compress_params(pe_k, w1_k, w2_k), _compress_params(pe_v, w1_v, w2_v),
                        tile2(k_g[0]), ind, *tabs_cmp)
    y_b = _nsa(z, qt, kc, vct, ks, vst, kw, vwt, gt, B, S)

    w_out16 = w_out.astype(bf)
    return _out_proj(x2, y_a, y_b, w_out16[:RG_WIDTH], w_out16[RG_WIDTH:])


def kernel(x, norm_g, w_in, conv_w, conv_b, rg_wr, rg_br, rg_wi, rg_bi, rg_lambda, q_norm_g, k_norm_g,
           cmp_pe_k, cmp_w1_k, cmp_w2_k, cmp_pe_v, cmp_w1_v, cmp_w2_v, w_out):
    B, S, D = x.shape
    assert D == D_MODEL and S % ROW_TILE == 0 and S % SEL_TILE == 0 and S >= WINDOW
    depth = norm_g.shape[0]
    tabs_tok = _rope_tables(jnp.arange(S))
    tabs_cmp = _rope_tables(jnp.arange(S // CMP_STRIDE) * CMP_STRIDE + (CMP_LEN - 1))
    ind = _block_diag(jnp.ones((LANE // HEAD_DIM, HEAD_DIM, HEAD_DIM), jnp.bfloat16))
    x2 = x.reshape(B * S, D)
    for l in range(depth):
        x2 = _layer(x2, B, S, norm_g[l], w_in[l], conv_w[l], conv_b[l], rg_wr[l], rg_br[l], rg_wi[l],
                    rg_bi[l], rg_lambda[l], q_norm_g[l], k_norm_g[l], cmp_pe_k[l], cmp_w1_k[l],
                    cmp_w2_k[l], cmp_pe_v[l], cmp_w1_v[l], cmp_w2_v[l], w_out[l], tabs_tok, tabs_cmp, ind)
    return x2.reshape(B, S, D)
```

```python
import functools
import math

import numpy as np
import jax
import jax.numpy as jnp
from jax import lax
from jax.experimental import pallas as pl
from jax.experimental.pallas import tpu as pltpu

D_MODEL = 1024
RG_WIDTH = 512
RG_BLOCKS = 8
RG_BLOCK = 64
CONV_W = 4
RG_C = 8.0
NSA_WIDTH = 512
HEAD_DIM = 64
N_KV = 2
HPG = 4
KV_WIDTH = N_KV * HEAD_DIM
ROPE_DIM = 16
ROPE_THETA = 500000.0
CMP_LEN = 32
CMP_STRIDE = 16
CMP_HIDDEN = 128
SEL_LEN = 64
SEL_TOPN = 16
WINDOW = 512
N_BRANCH = 3
Q_BLOCK = 128
EPS = 1e-6
NEG = -1e30
N_IN = 2 * RG_WIDTH + 2 * NSA_WIDTH + 6 * KV_WIDTH + N_BRANCH * 8
LANE = 128
N_PAD = ((N_IN + LANE - 1) // LANE) * LANE

COL_Q = 2 * RG_WIDTH
COL_KC = COL_Q + NSA_WIDTH
COL_VC = COL_KC + KV_WIDTH
COL_KS = COL_VC + KV_WIDTH
COL_VS = COL_KS + KV_WIDTH
COL_KW = COL_VS + KV_WIDTH
COL_VW = COL_KW + KV_WIDTH
COL_NG = COL_VW + KV_WIDTH
COL_BG = COL_NG + NSA_WIDTH

VMEM_LIMIT = 56 * 1024 * 1024

ROW_TILE = 512
SEL_TILE = 512
SEL_BLOCKS = SEL_TILE // SEL_LEN
WIN_TILE = 128
LOG2E = 1.4426950408889634


def _cparams(sem):
    return pltpu.CompilerParams(dimension_semantics=sem, vmem_limit_bytes=VMEM_LIMIT)


def _in_proj_body(x_ref, g_ref, w_ref, z_ref):
    x = x_ref[...]
    ms = jnp.mean(x * x, axis=-1, keepdims=True)
    h = (x * lax.rsqrt(ms + EPS)) * g_ref[...]
    z_ref[...] = jnp.dot(h.astype(jnp.bfloat16), w_ref[...], preferred_element_type=jnp.float32)


def _in_proj(x2, g, w):
    rows = x2.shape[0]
    return pl.pallas_call(
        _in_proj_body,
        out_shape=jax.ShapeDtypeStruct((rows, N_PAD), jnp.float32),
        grid=(rows // ROW_TILE,),
        in_specs=[
            pl.BlockSpec((ROW_TILE, D_MODEL), lambda i: (i, 0)),
            pl.BlockSpec((1, D_MODEL), lambda i: (0, 0)),
            pl.BlockSpec((D_MODEL, N_PAD), lambda i: (0, 0)),
        ],
        out_specs=pl.BlockSpec((ROW_TILE, N_PAD), lambda i: (i, 0)),
        compiler_params=_cparams(("arbitrary",)),
        name="in_proj",
    )(x2, g, w)


def _head_mean_sq(x, ind):
    sq = x * x
    hi = sq.astype(jnp.bfloat16)
    lo = (sq - hi.astype(jnp.float32)).astype(jnp.bfloat16)
    tot = (jnp.dot(hi, ind, preferred_element_type=jnp.float32)
           + jnp.dot(lo, ind, preferred_element_type=jnp.float32))
    return tot * (1.0 / HEAD_DIM)


def _norm_rope(x, gain, ind, ra, rm, rp):
    width = x.shape[-1]
    y = (x * lax.rsqrt(_head_mean_sq(x, ind) + EPS)) * gain
    up = pltpu.roll(y, width - ROPE_DIM // 2, axis=1)
    dn = pltpu.roll(y, ROPE_DIM // 2, axis=1)
    return y * ra + up * rm + dn * rp


def _prep_body(q_ref, ks_ref, vs_ref, kw_ref, vw_ref, bg_ref, ra_ref, rm_ref, rp_ref,
               qg_ref, kgs_ref, kgw_ref, ind_ref,
               qt_ref, kso_ref, vst_ref, kwo_ref, vwt_ref, gt_ref):
    ind = ind_ref[...]
    ra, rm, rp = ra_ref[...], rm_ref[...], rp_ref[...]
    scale = LOG2E / math.sqrt(HEAD_DIM)
    for c in range(NSA_WIDTH // LANE):
        sl = slice(c * LANE, (c + 1) * LANE)
        qn = _norm_rope(q_ref[:, sl], qg_ref[...], ind, ra, rm, rp) * scale
        qt_ref[0, sl, :] = qn.T.astype(jnp.bfloat16)
    kso_ref[0] = _norm_rope(ks_ref[...], kgs_ref[...], ind, ra, rm, rp).astype(jnp.bfloat16)
    kwo_ref[0] = _norm_rope(kw_ref[...], kgw_ref[...], ind, ra, rm, rp).astype(jnp.bfloat16)
    vst_ref[0] = vs_ref[...].T.astype(jnp.bfloat16)
    vwt_ref[0] = vw_ref[...].T.astype(jnp.bfloat16)
    gt_ref[0] = jax.nn.sigmoid(bg_ref[...]).T


def _prep(z, B, S, ra, rm, rp, qg, kgs, kgw, ind):
    nt = S // ROW_TILE

    def zspec(col, width):
        return pl.BlockSpec((ROW_TILE, width), lambda b, i: (b * nt + i, col // width))

    tab = pl.BlockSpec((ROW_TILE, LANE), lambda b, i: (i, 0))
    par = pl.BlockSpec((1, LANE), lambda b, i: (0, 0))
    bf = jnp.bfloat16
    return pl.pallas_call(
        _prep_body,
        out_shape=(
            jax.ShapeDtypeStruct((B, NSA_WIDTH, S), bf),
            jax.ShapeDtypeStruct((B, S, KV_WIDTH), bf),
            jax.ShapeDtypeStruct((B, KV_WIDTH, S), bf),
            jax.ShapeDtypeStruct((B, S, KV_WIDTH), bf),
            jax.ShapeDtypeStruct((B, KV_WIDTH, S), bf),
            jax.ShapeDtypeStruct((B, LANE, S), jnp.float32),
        ),
        grid=(B, nt),
        in_specs=[
            zspec(COL_Q, NSA_WIDTH), zspec(COL_KS, KV_WIDTH), zspec(COL_VS, KV_WIDTH),
            zspec(COL_KW, KV_WIDTH), zspec(COL_VW, KV_WIDTH), zspec(COL_BG, LANE),
            tab, tab, tab, par, par, par,
            pl.BlockSpec((LANE, LANE), lambda b, i: (0, 0)),
        ],
        out_specs=(
            pl.BlockSpec((1, NSA_WIDTH, ROW_TILE), lambda b, i: (b, 0, i)),
            pl.BlockSpec((1, ROW_TILE, KV_WIDTH), lambda b, i: (b, i, 0)),
            pl.BlockSpec((1, KV_WIDTH, ROW_TILE), lambda b, i: (b, 0, i)),
            pl.BlockSpec((1, ROW_TILE, KV_WIDTH), lambda b, i: (b, i, 0)),
            pl.BlockSpec((1, KV_WIDTH, ROW_TILE), lambda b, i: (b, 0, i)),
            pl.BlockSpec((1, LANE, ROW_TILE), lambda b, i: (b, 0, i)),
        ),
        compiler_params=_cparams(("arbitrary", "arbitrary")),
        name="prep",
    )(z, z, z, z, z, z, ra, rm, rp, qg, kgs, kgw, ind)


RG_TILE = 512
SUB = 8


def _rglru_body(x_ref, gate_ref, cw_ref, cb_ref, wr_ref, br_ref, wi_ref, bi_ref, lam_ref,
                y_ref, tail_ref, carry_ref, a_ref, b_ref):
    @pl.when(pl.program_id(1) == 0)
    def _():
        tail_ref[...] = jnp.zeros_like(tail_ref)
        carry_ref[...] = jnp.zeros_like(carry_ref)

    x = x_ref[...]
    tail = tail_ref[...]
    row8 = lax.broadcasted_iota(jnp.int32, (SUB, RG_WIDTH), 0)
    conv = x * cw_ref[CONV_W - 1:CONV_W, :] + cb_ref[...]
    for s in range(1, CONV_W):
        xs = pltpu.roll(x, s, axis=0)
        first = jnp.where(row8 < s, pltpu.roll(tail, s, axis=0), xs[0:SUB])
        xs = jnp.concatenate([first, xs[SUB:]], axis=0)
        conv = conv + xs * cw_ref[CONV_W - 1 - s:CONV_W - s, :]
    tail_ref[...] = x[RG_TILE - SUB:]

    cb16 = conv.astype(jnp.bfloat16)
    r = jax.nn.sigmoid(jnp.dot(cb16, wr_ref[...], preferred_element_type=jnp.float32) + br_ref[...])
    ig = jax.nn.sigmoid(jnp.dot(cb16, wi_ref[...], preferred_element_type=jnp.float32) + bi_ref[...])
    log_a = (-RG_C * r) * jax.nn.softplus(-lam_ref[...])
    a = jnp.exp(log_a)
    u = jnp.sqrt(-jnp.tanh(log_a) * (jnp.exp(2.0 * log_a) + 1.0)) * (ig * conv)

    row = lax.broadcasted_iota(jnp.int32, (RG_TILE, RG_WIDTH), 0) % SUB
    for d in (1, 2, 4):
        keep = row >= d
        a_s = jnp.where(keep, pltpu.roll(a, d, axis=0), 1.0)
        u_s = jnp.where(keep, pltpu.roll(u, d, axis=0), 0.0)
        u = a * u_s + u
        a = a * a_s
    a_ref[...] = a
    b_ref[...] = u

    def step(k, h):
        off = pl.multiple_of(k * SUB, SUB)
        hk = a_ref[pl.ds(off, SUB), :] * h + b_ref[pl.ds(off, SUB), :]
        b_ref[pl.ds(off, SUB), :] = hk
        return jnp.broadcast_to(hk[SUB - 1:SUB, :], (SUB, RG_WIDTH))

    carry_ref[...] = lax.fori_loop(0, RG_TILE // SUB, step, carry_ref[...])
    gate = gate_ref[...]
    y_ref[...] = (b_ref[...] * (gate * jax.nn.sigmoid(gate))).astype(y_ref.dtype)


def _rglru(z, B, S, cw, cb, wr, br, wi, bi, lam):
    nt = S // RG_TILE
    vec = pl.BlockSpec((1, RG_WIDTH), lambda b, i: (0, 0))
    mat = pl.BlockSpec((RG_WIDTH, RG_WIDTH), lambda b, i: (0, 0))
    return pl.pallas_call(
        _rglru_body,
        out_shape=jax.ShapeDtypeStruct((B * S, RG_WIDTH), jnp.bfloat16),
        grid=(B, nt),
        in_specs=[
            pl.BlockSpec((RG_TILE, RG_WIDTH), lambda b, i: (b * nt + i, 0)),
            pl.BlockSpec((RG_TILE, RG_WIDTH), lambda b, i: (b * nt + i, 1)),
            pl.BlockSpec((CONV_W, RG_WIDTH), lambda b, i: (0, 0)),
            vec, mat, vec, mat, vec, vec,
        ],
        out_specs=pl.BlockSpec((RG_TILE, RG_WIDTH), lambda b, i: (b * nt + i, 0)),
        scratch_shapes=[
            pltpu.VMEM((SUB, RG_WIDTH), jnp.float32),
            pltpu.VMEM((SUB, RG_WIDTH), jnp.float32),
            pltpu.VMEM((RG_TILE, RG_WIDTH), jnp.float32),
            pltpu.VMEM((RG_TILE, RG_WIDTH), jnp.float32),
        ],
        compiler_params=_cparams(("arbitrary", "arbitrary")),
        name="rglru",
    )(z, z, cw, cb, wr, br, wi, bi, lam)


N_CHUNK_TOK = CMP_STRIDE


def _compress_one(src_ref, wa_ref, wb_ref, pea_ref, peb_ref, w2_ref, n_chunk):
    ha = jnp.zeros((n_chunk, 2 * CMP_HIDDEN), jnp.float32)
    hb = jnp.zeros((n_chunk, 2 * CMP_HIDDEN), jnp.float32)
    for j in range(N_CHUNK_TOK):
        xj = src_ref[pl.ds(j, n_chunk, stride=N_CHUNK_TOK), :]
        ha = ha + jnp.dot((xj + pea_ref[j]).astype(jnp.bfloat16), wa_ref[j],
                          preferred_element_type=jnp.float32)
        hb = hb + jnp.dot((xj + peb_ref[j]).astype(jnp.bfloat16), wb_ref[j],
                          preferred_element_type=jnp.float32)
    hid = ha + pltpu.roll(hb, n_chunk - 1, axis=0)
    act = hid * jax.nn.sigmoid(hid)
    return jnp.dot(act.astype(jnp.bfloat16), w2_ref[...], preferred_element_type=jnp.float32)


def _compress_body(kc_ref, vc_ref, wak_ref, wbk_ref, peak_ref, pebk_ref, w2k_ref,
                   wav_ref, wbv_ref, peav_ref, pebv_ref, w2v_ref,
                   kg_ref, ind_ref, ra_ref, rm_ref, rp_ref, ko_ref, vto_ref, *, n_chunk):
    kc = _compress_one(kc_ref, wak_ref, wbk_ref, peak_ref, pebk_ref, w2k_ref, n_chunk)
    ko_ref[0] = _norm_rope(kc, kg_ref[...], ind_ref[...], ra_ref[...], rm_ref[...],
                           rp_ref[...]).astype(jnp.bfloat16)
    vc = _compress_one(vc_ref, wav_ref, wbv_ref, peav_ref, pebv_ref, w2v_ref, n_chunk)
    vto_ref[0] = vc.T.astype(jnp.bfloat16)


def _compress(z, B, S, wk, wv, kg, ind, ra, rm, rp):
    n_chunk = S // N_CHUNK_TOK

    def full(a):
        nd = a.ndim
        return pl.BlockSpec(a.shape, lambda b: (0,) * nd)

    consts = list(wk) + list(wv) + [kg, ind, ra, rm, rp]
    return pl.pallas_call(
        functools.partial(_compress_body, n_chunk=n_chunk),
        out_shape=(
            jax.ShapeDtypeStruct((B, n_chunk, KV_WIDTH), jnp.bfloat16),
            jax.ShapeDtypeStruct((B, KV_WIDTH, n_chunk), jnp.bfloat16),
        ),
        grid=(B,),
        in_specs=[
            pl.BlockSpec((S, KV_WIDTH), lambda b: (b, COL_KC // KV_WIDTH)),
            pl.BlockSpec((S, KV_WIDTH), lambda b: (b, COL_VC // KV_WIDTH)),
        ] + [full(a) for a in consts],
        out_specs=(
            pl.BlockSpec((1, n_chunk, KV_WIDTH), lambda b: (b, 0, 0)),
            pl.BlockSpec((1, KV_WIDTH, n_chunk), lambda b: (b, 0, 0)),
        ),
        compiler_params=_cparams(("arbitrary",)),
        name="compress",
    )(z, z, *consts)


G_COL = HPG * Q_BLOCK
N_COL = N_KV * G_COL


def _group_dots(vts, p):
    p16 = p.astype(jnp.bfloat16)
    return jnp.concatenate(
        [jnp.dot(vt, p16[:, g * G_COL:(g + 1) * G_COL], preferred_element_type=jnp.float32)
         for g, vt in enumerate(vts)], axis=1)


def _tile_softmax(s, vts):
    m = jnp.max(s, axis=0, keepdims=True)
    p = jnp.exp2(s - m)
    l = jnp.sum(p, axis=0, keepdims=True)
    return m, l, _group_dots(vts, p)


def _merge(state, tile):
    m0, l0, acc0 = state
    m1, l1, pv1 = tile
    m = jnp.maximum(m0, m1)
    a, b = jnp.exp2(m0 - m), jnp.exp2(m1 - m)
    return m, a * l0 + b * l1, a * acc0 + b * pv1


def _finish(state):
    _, l, acc = state
    return acc * jnp.where(l > 0.0, 1.0 / l, 0.0)


def _nsa_body(qt_ref, kc_ref, vct_ref, ks_ref, vst_ref, kw_ref, vwt_ref, gt_ref, ng0_ref, ng1_ref, e_ref,
              y_ref, ps_ref, bias_ref, sa_ref, sb_ref, *, n_cmp_pad):
    qi = pl.program_id(1)
    q0 = qi * Q_BLOCK
    groups = range(N_KV)
    grows = lambda g: slice(g * HEAD_DIM, (g + 1) * HEAD_DIM)
    gcols = lambda g: slice(g * G_COL, (g + 1) * G_COL)

    qt = qt_ref[0]
    zero = jnp.zeros((HEAD_DIM, G_COL), jnp.bfloat16)
    qrows = []
    for g in groups:
        qcat = jnp.concatenate([qt[(g * HPG + h) * HEAD_DIM:(g * HPG + h + 1) * HEAD_DIM, :]
                                for h in range(HPG)], axis=1)
        qrows.append(jnp.concatenate([qcat if gg == g else zero for gg in groups], axis=1))
    qz = jnp.concatenate(qrows, axis=0)

    tl = lax.broadcasted_iota(jnp.int32, (1, N_COL), 1) % Q_BLOCK
    tcol = q0 + tl

    sc = jnp.dot(kc_ref[0], qz, preferred_element_type=jnp.float32)
    crow = lax.broadcasted_iota(jnp.int32, (n_cmp_pad, N_COL), 0)
    cvalid = crow <= (tcol - (CMP_LEN - 1)) // CMP_STRIDE
    scm = jnp.where(cvalid, sc, NEG)
    mc = jnp.max(scm, axis=0, keepdims=True)
    pc = jnp.exp2(scm - mc)
    lc = jnp.sum(pc, axis=0, keepdims=True)
    pc = pc * jnp.where(tcol >= CMP_LEN - 1, 1.0 / lc, 0.0)
    o_c = _group_dots([vct_ref[0, grows(g), :] for g in groups], pc)

    n_sel = n_cmp_pad // 4
    per = SEL_LEN // CMP_STRIDE
    tq = q0 + lax.broadcasted_iota(jnp.int32, (n_sel, Q_BLOCK), 1)
    bt = tq // SEL_LEN
    rowj = lax.broadcasted_iota(jnp.int32, (n_sel, Q_BLOCK), 0)
    forced = (rowj == 0) | (rowj == bt) | (rowj == bt - 1)
    valid = rowj <= bt
    rowf = rowj.astype(jnp.float32)
    cands, biases = [], []
    for g in groups:
        psum = pc[:, g * G_COL:g * G_COL + Q_BLOCK]
        for h in range(1, HPG):
            psum = psum + pc[:, g * G_COL + h * Q_BLOCK:g * G_COL + (h + 1) * Q_BLOCK]
        ps_ref[g, 0:SUB, :] = jnp.zeros((SUB, Q_BLOCK), jnp.float32)
        ps_ref[g, SUB:, :] = psum
        strided = lambda k: ps_ref[g, pl.ds(SUB + k, n_sel, stride=per), :]
        imp = strided(0) + strided(1) + strided(2) + 0.5 * strided(3) + 0.5 * strided(-1)
        cands.append(jnp.where(valid & jnp.logical_not(forced), imp, -1.0))
        biases.append(jnp.where(forced & valid, 0.0, NEG))

    for _ in range(SEL_TOPN - 3):
        for g in groups:
            mx = jnp.max(cands[g], axis=0, keepdims=True)
            first = jnp.min(jnp.where(cands[g] == mx, rowf, float(n_sel)), axis=0, keepdims=True)
            hit = (rowf == first) & (mx >= 0.0)
            cands[g] = jnp.where(hit, -1.0, cands[g])
            biases[g] = jnp.where(hit, 0.0, biases[g])
    bias_ref[0:n_sel, :] = jnp.concatenate(biases, axis=1)
    bias_ref[n_sel:, :] = jnp.zeros((LANE, N_KV * Q_BLOCK), jnp.float32)

    win_len = WINDOW + Q_BLOCK
    woff = pl.multiple_of(q0, Q_BLOCK)
    flag = jnp.where(lax.broadcasted_iota(jnp.int32, (LANE, N_COL), 0) == 0, NEG, 0.0)
    sw = jnp.dot(kw_ref[0, pl.ds(woff, win_len), :],
                 jnp.concatenate([qz, flag.astype(jnp.bfloat16)], axis=0),
                 preferred_element_type=jnp.float32)
    wrow = lax.broadcasted_iota(jnp.int32, (Q_BLOCK, N_COL), 0)
    sw = jnp.concatenate([jnp.where(wrow > tl, sw[0:Q_BLOCK], NEG),
                          sw[Q_BLOCK:WINDOW],
                          jnp.where(wrow <= tl, sw[WINDOW:], NEG)], axis=0)
    mw, lw, pvw = _tile_softmax(sw, [vwt_ref[0, grows(g), pl.ds(woff, win_len)] for g in groups])
    o_w = pvw * (1.0 / lw)

    e_mat = e_ref[...]

    def sel_scores(t):
        koff = pl.multiple_of(t * SEL_TILE, SEL_TILE)
        boff = pl.multiple_of(t * SEL_BLOCKS, SEL_BLOCKS)
        brows = bias_ref[pl.ds(boff, LANE), :].astype(jnp.bfloat16)
        bcols = jnp.concatenate([brows[:, g * Q_BLOCK:(g + 1) * Q_BLOCK] for g in groups for _ in range(HPG)],
                                axis=1)
        rhs = jnp.concatenate([qz, bcols], axis=0)
        lhs = jnp.concatenate([ks_ref[0, pl.ds(koff, SEL_TILE), :], e_mat], axis=1)
        return jnp.dot(lhs, rhs, preferred_element_type=jnp.float32)

    def sel_values(t):
        koff = pl.multiple_of(t * SEL_TILE, SEL_TILE)
        return [vst_ref[0, grows(g), pl.ds(koff, SEL_TILE)] for g in groups]

    def pair(j, state):
        t = 2 * j
        sb_ref[...] = sel_scores(t + 1)
        state = _merge(state, _tile_softmax(sa_ref[...], sel_values(t)))
        sa_ref[...] = sel_scores(t + 2)
        return _merge(state, _tile_softmax(sb_ref[...], sel_values(t + 1)))

    def last_tile(s_ref, t, state):
        srow = lax.broadcasted_iota(jnp.int32, (SEL_TILE, N_COL), 0)
        s = jnp.where(srow <= tcol - t * SEL_TILE, s_ref[...], NEG)
        return _finish(_merge(state, _tile_softmax(s, sel_values(t))))

    n_open = q0 // SEL_TILE
    sa_ref[...] = sel_scores(0)
    state = (jnp.full((1, N_COL), NEG, jnp.float32), jnp.zeros((1, N_COL), jnp.float32),
             jnp.zeros((HEAD_DIM, N_COL), jnp.float32))
    state = lax.fori_loop(0, n_open // 2, pair, state)
    t_next = 2 * (n_open // 2)

    def odd_tail(state):
        sb_ref[...] = sel_scores(t_next + 1)
        state = _merge(state, _tile_softmax(sa_ref[...], sel_values(t_next)))
        return last_tile(sb_ref, t_next + 1, state)

    def even_tail(state):
        return last_tile(sa_ref, t_next, state)

    o_s = lax.cond(n_open % 2 == 1, odd_tail, even_tail, state)

    def gate_row(branch):
        return jnp.concatenate(
            [gt_ref[0, r * N_BRANCH + branch:r * N_BRANCH + branch + 1, :] for r in range(N_KV * HPG)], axis=1)

    yt = gate_row(0) * o_c + gate_row(1) * o_s + gate_row(2) * o_w
    halves = []
    for pr in range(N_KV * HPG // 2):
        two = jnp.concatenate([yt[:, (2 * pr) * Q_BLOCK:(2 * pr + 1) * Q_BLOCK],
                               yt[:, (2 * pr + 1) * Q_BLOCK:(2 * pr + 2) * Q_BLOCK]], axis=0)
        halves.append(two.T)
    y = jnp.concatenate(halves, axis=1)
    ng = jnp.concatenate([ng0_ref[...], ng1_ref[...]], axis=1)
    y_ref[...] = (y * (ng * jax.nn.sigmoid(ng))).astype(y_ref.dtype)


def _nsa(z, qt, kc, vct, ks, vst, kw, vwt, gt, B, S):
    nq = S // Q_BLOCK
    n_cmp_pad = S // CMP_STRIDE
    gw = HPG * HEAD_DIM
    bf = jnp.bfloat16
    e_mat = (jnp.arange(SEL_TILE)[:, None] // SEL_LEN == jnp.arange(LANE)[None, :]).astype(bf)
    lead = jnp.zeros((B, WINDOW, 2 * KV_WIDTH), bf).at[:, :, KV_WIDTH].set(1.0)
    kw_pad = jnp.concatenate([lead, jnp.concatenate([kw, jnp.zeros_like(kw)], axis=2)], axis=1)
    vwt_pad = jnp.pad(vwt, ((0, 0), (0, 0), (WINDOW, 0)))
    whole = lambda a: pl.BlockSpec((1,) + a.shape[1:], lambda b, i: (b, 0, 0))
    return pl.pallas_call(
        functools.partial(_nsa_body, n_cmp_pad=n_cmp_pad),
        out_shape=jax.ShapeDtypeStruct((B * S, NSA_WIDTH), bf),
        grid=(B, nq),
        in_specs=[
            pl.BlockSpec((1, NSA_WIDTH, Q_BLOCK), lambda b, i: (b, 0, i)),
            whole(kc), whole(vct), whole(ks), whole(vst), whole(kw_pad), whole(vwt_pad),
            pl.BlockSpec((1, LANE, Q_BLOCK), lambda b, i: (b, 0, i)),
            pl.BlockSpec((Q_BLOCK, gw), lambda b, i: (b * nq + i, COL_NG // gw)),
            pl.BlockSpec((Q_BLOCK, gw), lambda b, i: (b * nq + i, COL_NG // gw + 1)),
            pl.BlockSpec((SEL_TILE, LANE), lambda b, i: (0, 0)),
        ],
        out_specs=pl.BlockSpec((Q_BLOCK, NSA_WIDTH), lambda b, i: (b * nq + i, 0)),
        scratch_shapes=[
            pltpu.VMEM((N_KV, SUB + n_cmp_pad, Q_BLOCK), jnp.float32),
            pltpu.VMEM((S // SEL_LEN + LANE, N_KV * Q_BLOCK), jnp.float32),
            pltpu.VMEM((SEL_TILE, N_COL), jnp.float32),
            pltpu.VMEM((SEL_TILE, N_COL), jnp.float32),
        ],
        compiler_params=_cparams(("arbitrary", "arbitrary")),
        name="nsa",
    )(qt, kc, vct, ks, vst, kw_pad, vwt_pad, gt, z, z, e_mat)


def _out_proj_body(x_ref, ya_ref, yb_ref, wa_ref, wb_ref, o_ref):
    o_ref[...] = (x_ref[...]
                  + jnp.dot(ya_ref[...], wa_ref[...], preferred_element_type=jnp.float32)
                  + jnp.dot(yb_ref[...], wb_ref[...], preferred_element_type=jnp.float32))


def _out_proj(x2, ya, yb, wa, wb):
    rows = x2.shape[0]
    return pl.pallas_call(
        _out_proj_body,
        out_shape=jax.ShapeDtypeStruct((rows, D_MODEL), jnp.float32),
        grid=(rows // ROW_TILE,),
        in_specs=[
            pl.BlockSpec((ROW_TILE, D_MODEL), lambda i: (i, 0)),
            pl.BlockSpec((ROW_TILE, RG_WIDTH), lambda i: (i, 0)),
            pl.BlockSpec((ROW_TILE, NSA_WIDTH), lambda i: (i, 0)),
            pl.BlockSpec((RG_WIDTH, D_MODEL), lambda i: (0, 0)),
            pl.BlockSpec((NSA_WIDTH, D_MODEL), lambda i: (0, 0)),
        ],
        out_specs=pl.BlockSpec((ROW_TILE, D_MODEL), lambda i: (i, 0)),
        compiler_params=_cparams(("arbitrary",)),
        name="out_proj",
    )(x2, ya, yb, wa, wb)


def _rope_tables(pos):
    half = ROPE_DIM // 2
    inv = ROPE_THETA ** (-jnp.arange(half, dtype=jnp.float32) / half)
    ang = pos.astype(jnp.float32)[:, None] * inv[None, :]
    cos, sin = jnp.cos(ang), jnp.sin(ang)
    n = pos.shape[0]
    rest = HEAD_DIM - ROPE_DIM
    ra = jnp.concatenate([cos, cos, jnp.ones((n, rest), jnp.float32)], axis=1)
    rm = jnp.concatenate([-sin, jnp.zeros((n, half + rest), jnp.float32)], axis=1)
    rp = jnp.concatenate([jnp.zeros((n, half), jnp.float32), sin, jnp.zeros((n, rest), jnp.float32)], axis=1)
    return tuple(jnp.tile(t, (1, LANE // HEAD_DIM)) for t in (ra, rm, rp))


def _block_diag(blocks):
    n, r, c = blocks.shape
    eye = jnp.eye(n, dtype=blocks.dtype)
    return (eye[:, None, :, None] * blocks[:, :, None, :]).reshape(n * r, n * c)


def _compress_params(pe, w1, w2):
    bf = jnp.bfloat16
    w1t = w1.reshape(2, N_CHUNK_TOK, HEAD_DIM, CMP_HIDDEN)
    dup = lambda w: jax.vmap(lambda m: _block_diag(jnp.stack([m] * N_KV)))(w)
    wa, wb = dup(w1t[0]).astype(bf), dup(w1t[1]).astype(bf)
    pet = jnp.tile(pe.reshape(2, N_CHUNK_TOK, 1, HEAD_DIM), (1, 1, 1, N_KV))
    w2d = _block_diag(jnp.stack([w2] * N_KV)).astype(bf)
    return wa, wb, pet[0], pet[1], w2d


def _layer(x2, B, S, norm_g, w_in, conv_w, conv_b, rg_wr, rg_br, rg_wi, rg_bi, rg_lam,
           q_g, k_g, pe_k, w1_k, w2_k, pe_v, w1_v, w2_v, w_out, tabs_tok, tabs_cmp, ind):
    bf = jnp.bfloat16
    w_in_p = jnp.pad(w_in, ((0, 0), (0, N_PAD - N_IN))).astype(bf)
    z = _in_proj(x2, norm_g[None, :], w_in_p)

    y_a = _rglru(z, B, S, conv_w, conv_b[None, :], _block_diag(rg_wr).astype(bf), rg_br[None, :],
                 _block_diag(rg_wi).astype(bf), rg_bi[None, :], rg_lam[None, :])

    tile2 = lambda v: jnp.tile(v, LANE // HEAD_DIM)[None, :]
    qt, ks, vst, kw, vwt, gt = _prep(z, B, S, *tabs_tok, tile2(q_g), tile2(k_g[1]), tile2(k_g[2]), ind)
    kc, vct = _compress(z, B, S, _compress_params(pe_k, w1_k, w2_k), _compress_params(pe_v, w1_v, w2_v),
                        tile2(k_g[0]), ind, *tabs_cmp)
    y_b = _nsa(z, qt, kc, vct, ks, vst, kw, vwt, gt, B, S)

    w_out16 = w_out.astype(bf)
    return _out_proj(x2, y_a, y_b, w_out16[:RG_WIDTH], w_out16[RG_WIDTH:])


def kernel(x, norm_g, w_in, conv_w, conv_b, rg_wr, rg_br, rg_wi, rg_bi, rg_lambda, q_norm_g, k_norm_g,
           cmp_pe_k, cmp_w1_k, cmp_w2_k, cmp_pe_v, cmp_w1_v, cmp_w2_v, w_out):
    B, S, D = x.shape
    assert D == D_MODEL and S % ROW_TILE == 0 and S % SEL_TILE == 0 and S >= WINDOW
    depth = norm_g.shape[0]
    tabs_tok = _rope_tables(jnp.arange(S))
    tabs_cmp = _rope_tables(jnp.arange(S // CMP_STRIDE) * CMP_STRIDE + (CMP_LEN - 1))
    ind = _block_diag(jnp.ones((LANE // HEAD_DIM, HEAD_DIM, HEAD_DIM), jnp.bfloat16))
    x2 = x.reshape(B * S, D)
    for l in range(depth):
        x2 = _layer(x2, B, S, norm_g[l], w_in[l], conv_w[l], conv_b[l], rg_wr[l], rg_br[l], rg_wi[l],
                    rg_bi[l], rg_lambda[l], q_norm_g[l], k_norm_g[l], cmp_pe_k[l], cmp_w1_k[l],
                    cmp_w2_k[l], cmp_pe_v[l], cmp_w1_v[l], cmp_w2_v[l], w_out[l], tabs_tok, tabs_cmp, ind)
    return x2.reshape(B, S, D)
```

```python
import functools
import math

import numpy as np
import jax
import jax.numpy as jnp
from jax import lax
from jax.experimental import pallas as pl
from jax.experimental.pallas import tpu as pltpu

D_MODEL = 1024
RG_WIDTH = 512
RG_BLOCKS = 8
RG_BLOCK = 64
CONV_W = 4
RG_C = 8.0
NSA_WIDTH = 512
HEAD_DIM = 64
N_KV = 2
HPG = 4
KV_WIDTH = N_KV * HEAD_DIM
ROPE_DIM = 16
ROPE_THETA = 500000.0
CMP_LEN = 32
CMP_STRIDE = 16
CMP_HIDDEN = 128
SEL_LEN = 64
SEL_TOPN = 16
WINDOW = 512
N_BRANCH = 3
Q_BLOCK = 128
EPS = 1e-6
NEG = -1e30
N_IN = 2 * RG_WIDTH + 2 * NSA_WIDTH + 6 * KV_WIDTH + N_BRANCH * 8
LANE = 128
N_PAD = ((N_IN + LANE - 1) // LANE) * LANE

COL_Q = 2 * RG_WIDTH
COL_KC = COL_Q + NSA_WIDTH
COL_VC = COL_KC + KV_WIDTH
COL_KS = COL_VC + KV_WIDTH
COL_VS = COL_KS + KV_WIDTH
COL_KW = COL_VS + KV_WIDTH
COL_VW = COL_KW + KV_WIDTH
COL_NG = COL_VW + KV_WIDTH
COL_BG = COL_NG + NSA_WIDTH

VMEM_LIMIT = 56 * 1024 * 1024

ROW_TILE = 512
SEL_TILE = 512
SEL_BLOCKS = SEL_TILE // SEL_LEN
WIN_TILE = 128
LOG2E = 1.4426950408889634


def _cparams(sem):
    return pltpu.CompilerParams(dimension_semantics=sem, vmem_limit_bytes=VMEM_LIMIT)


def _in_proj_body(x_ref, g_ref, w_ref, z_ref):
    x = x_ref[...]
    ms = jnp.mean(x * x, axis=-1, keepdims=True)
    h = (x * lax.rsqrt(ms + EPS)) * g_ref[...]
    z_ref[...] = jnp.dot(h.astype(jnp.bfloat16), w_ref[...], preferred_element_type=jnp.float32)


def _in_proj(x2, g, w):
    rows = x2.shape[0]
    return pl.pallas_call(
        _in_proj_body,
        out_shape=jax.ShapeDtypeStruct((rows, N_PAD), jnp.float32),
        grid=(rows // ROW_TILE,),
        in_specs=[
            pl.BlockSpec((ROW_TILE, D_MODEL), lambda i: (i, 0)),
            pl.BlockSpec((1, D_MODEL), lambda i: (0, 0)),
            pl.BlockSpec((D_MODEL, N_PAD), lambda i: (0, 0)),
        ],
        out_specs=pl.BlockSpec((ROW_TILE, N_PAD), lambda i: (i, 0)),
        compiler_params=_cparams(("arbitrary",)),
        name="in_proj",
    )(x2, g, w)


def _head_mean_sq(x, ind):
    sq = x * x
    hi = sq.astype(jnp.bfloat16)
    lo = (sq - hi.astype(jnp.float32)).astype(jnp.bfloat16)
    tot = (jnp.dot(hi, ind, preferred_element_type=jnp.float32)
           + jnp.dot(lo, ind, preferred_element_type=jnp.float32))
    return tot * (1.0 / HEAD_DIM)


def _norm_rope(x, gain, ind, ra, rm, rp):
    width = x.shape[-1]
    y = (x * lax.rsqrt(_head_mean_sq(x, ind) + EPS)) * gain
    up = pltpu.roll(y, width - ROPE_DIM // 2, axis=1)
    dn = pltpu.roll(y, ROPE_DIM // 2, axis=1)
    return y * ra + up * rm + dn * rp


def _prep_body(q_ref, ks_ref, vs_ref, kw_ref, vw_ref, bg_ref, ra_ref, rm_ref, rp_ref,
               qg_ref, kgs_ref, kgw_ref, ind_ref,
               qt_ref, kso_ref, vst_ref, kwo_ref, vwt_ref, gt_ref):
    ind = ind_ref[...]
    ra, rm, rp = ra_ref[...], rm_ref[...], rp_ref[...]
    scale = LOG2E / math.sqrt(HEAD_DIM)
    for c in range(NSA_WIDTH // LANE):
        sl = slice(c * LANE, (c + 1) * LANE)
        qn = _norm_rope(q_ref[:, sl], qg_ref[...], ind, ra, rm, rp) * scale
        qt_ref[0, sl, :] = qn.T.astype(jnp.bfloat16)
    kso_ref[0] = _norm_rope(ks_ref[...], kgs_ref[...], ind, ra, rm, rp).astype(jnp.bfloat16)
    kwo_ref[0] = _norm_rope(kw_ref[...], kgw_ref[...], ind, ra, rm, rp).astype(jnp.bfloat16)
    vst_ref[0] = vs_ref[...].T.astype(jnp.bfloat16)
    vwt_ref[0] = vw_ref[...].T.astype(jnp.bfloat16)
    gt_ref[0] = jax.nn.sigmoid(bg_ref[...]).T


def _prep(z, B, S, ra, rm, rp, qg, kgs, kgw, ind):
    nt = S // ROW_TILE

    def zspec(col, width):
        return pl.BlockSpec((ROW_TILE, width), lambda b, i: (b * nt + i, col // width))

    tab = pl.BlockSpec((ROW_TILE, LANE), lambda b, i: (i, 0))
    par = pl.BlockSpec((1, LANE), lambda b, i: (0, 0))
    bf = jnp.bfloat16
    return pl.pallas_call(
        _prep_body,
        out_shape=(
            jax.ShapeDtypeStruct((B, NSA_WIDTH, S), bf),
            jax.ShapeDtypeStruct((B, S, KV_WIDTH), bf),
            jax.ShapeDtypeStruct((B, KV_WIDTH, S), bf),
            jax.ShapeDtypeStruct((B, S, KV_WIDTH), bf),
            jax.ShapeDtypeStruct((B, KV_WIDTH, S), bf),
            jax.ShapeDtypeStruct((B, LANE, S), jnp.float32),
        ),
        grid=(B, nt),
        in_specs=[
            zspec(COL_Q, NSA_WIDTH), zspec(COL_KS, KV_WIDTH), zspec(COL_VS, KV_WIDTH),
            zspec(COL_KW, KV_WIDTH), zspec(COL_VW, KV_WIDTH), zspec(COL_BG, LANE),
            tab, tab, tab, par, par, par,
            pl.BlockSpec((LANE, LANE), lambda b, i: (0, 0)),
        ],
        out_specs=(
            pl.BlockSpec((1, NSA_WIDTH, ROW_TILE), lambda b, i: (b, 0, i)),
            pl.BlockSpec((1, ROW_TILE, KV_WIDTH), lambda b, i: (b, i, 0)),
            pl.BlockSpec((1, KV_WIDTH, ROW_TILE), lambda b, i: (b, 0, i)),
            pl.BlockSpec((1, ROW_TILE, KV_WIDTH), lambda b, i: (b, i, 0)),
            pl.BlockSpec((1, KV_WIDTH, ROW_TILE), lambda b, i: (b, 0, i)),
            pl.BlockSpec((1, LANE, ROW_TILE), lambda b, i: (b, 0, i)),
        ),
        compiler_params=_cparams(("arbitrary", "arbitrary")),
        name="prep",
    )(z, z, z, z, z, z, ra, rm, rp, qg, kgs, kgw, ind)


RG_TILE = 512
SUB = 8


def _rglru_body(x_ref, gate_ref, cw_ref, cb_ref, wr_ref, br_ref, wi_ref, bi_ref, lam_ref,
                y_ref, tail_ref, carry_ref, a_ref, b_ref):
    @pl.when(pl.program_id(1) == 0)
    def _():
        tail_ref[...] = jnp.zeros_like(tail_ref)
        carry_ref[...] = jnp.zeros_like(carry_ref)

    x = x_ref[...]
    tail = tail_ref[...]
    row8 = lax.broadcasted_iota(jnp.int32, (SUB, RG_WIDTH), 0)
    conv = x * cw_ref[CONV_W - 1:CONV_W, :] + cb_ref[...]
    for s in range(1, CONV_W):
        xs = pltpu.roll(x, s, axis=0)
        first = jnp.where(row8 < s, pltpu.roll(tail, s, axis=0), xs[0:SUB])
        xs = jnp.concatenate([first, xs[SUB:]], axis=0)
        conv = conv + xs * cw_ref[CONV_W - 1 - s:CONV_W - s, :]
    tail_ref[...] = x[RG_TILE - SUB:]

    cb16 = conv.astype(jnp.bfloat16)
    r = jax.nn.sigmoid(jnp.dot(cb16, wr_ref[...], preferred_element_type=jnp.float32) + br_ref[...])
    ig = jax.nn.sigmoid(jnp.dot(cb16, wi_ref[...], preferred_element_type=jnp.float32) + bi_ref[...])
    log_a = (-RG_C * r) * jax.nn.softplus(-lam_ref[...])
    a = jnp.exp(log_a)
    u = jnp.sqrt(-jnp.tanh(log_a) * (a * a + 1.0)) * (ig * conv)

    grp = (RG_TILE // SUB, SUB, RG_WIDTH)
    a, u = a.reshape(grp), u.reshape(grp)
    row = lax.broadcasted_iota(jnp.int32, grp, 1)
    for d in (1, 2, 4):
        keep = row >= d
        a_s = jnp.where(keep, pltpu.roll(a, d, axis=1), 1.0)
        u_s = jnp.where(keep, pltpu.roll(u, d, axis=1), 0.0)
        u = a * u_s + u
        a = a * a_s
    a_ref[...] = a.reshape(RG_TILE, RG_WIDTH)
    b_ref[...] = u.reshape(RG_TILE, RG_WIDTH)

    def step(k, h):
        off = pl.multiple_of(k * SUB, SUB)
        hk = a_ref[pl.ds(off, SUB), :] * h + b_ref[pl.ds(off, SUB), :]
        b_ref[pl.ds(off, SUB), :] = hk
        return jnp.broadcast_to(hk[SUB - 1:SUB, :], (SUB, RG_WIDTH))

    carry_ref[...] = lax.fori_loop(0, RG_TILE // SUB, step, carry_ref[...])
    gate = gate_ref[...]
    y_ref[...] = (b_ref[...] * (gate * jax.nn.sigmoid(gate))).astype(y_ref.dtype)


def _rglru(z, B, S, cw, cb, wr, br, wi, bi, lam):
    nt = S // RG_TILE
    vec = pl.BlockSpec((1, RG_WIDTH), lambda b, i: (0, 0))
    mat = pl.BlockSpec((RG_WIDTH, RG_WIDTH), lambda b, i: (0, 0))
    return pl.pallas_call(
        _rglru_body,
        out_shape=jax.ShapeDtypeStruct((B * S, RG_WIDTH), jnp.bfloat16),
        grid=(B, nt),
        in_specs=[
            pl.BlockSpec((RG_TILE, RG_WIDTH), lambda b, i: (b * nt + i, 0)),
            pl.BlockSpec((RG_TILE, RG_WIDTH), lambda b, i: (b * nt + i, 1)),
            pl.BlockSpec((CONV_W, RG_WIDTH), lambda b, i: (0, 0)),
            vec, mat, vec, mat, vec, vec,
        ],
        out_specs=pl.BlockSpec((RG_TILE, RG_WIDTH), lambda b, i: (b * nt + i, 0)),
        scratch_shapes=[
            pltpu.VMEM((SUB, RG_WIDTH), jnp.float32),
            pltpu.VMEM((SUB, RG_WIDTH), jnp.float32),
            pltpu.VMEM((RG_TILE, RG_WIDTH), jnp.float32),
            pltpu.VMEM((RG_TILE, RG_WIDTH), jnp.float32),
        ],
        compiler_params=_cparams(("arbitrary", "arbitrary")),
        name="rglru",
    )(z, z, cw, cb, wr, br, wi, bi, lam)


N_CHUNK_TOK = CMP_STRIDE


def _compress_one(src_ref, wa_ref, wb_ref, pea_ref, peb_ref, w2_ref, n_chunk):
    ha = jnp.zeros((n_chunk, 2 * CMP_HIDDEN), jnp.float32)
    hb = jnp.zeros((n_chunk, 2 * CMP_HIDDEN), jnp.float32)
    for j in range(N_CHUNK_TOK):
        xj = src_ref[pl.ds(j, n_chunk, stride=N_CHUNK_TOK), :]
        ha = ha + jnp.dot((xj + pea_ref[j]).astype(jnp.bfloat16), wa_ref[j],
                          preferred_element_type=jnp.float32)
        hb = hb + jnp.dot((xj + peb_ref[j]).astype(jnp.bfloat16), wb_ref[j],
                          preferred_element_type=jnp.float32)
    hid = ha + pltpu.roll(hb, n_chunk - 1, axis=0)
    act = hid * jax.nn.sigmoid(hid)
    return jnp.dot(act.astype(jnp.bfloat16), w2_ref[...], preferred_element_type=jnp.float32)


def _compress_body(kc_ref, vc_ref, wak_ref, wbk_ref, peak_ref, pebk_ref, w2k_ref,
                   wav_ref, wbv_ref, peav_ref, pebv_ref, w2v_ref,
                   kg_ref, ind_ref, ra_ref, rm_ref, rp_ref, ko_ref, vto_ref, *, n_chunk):
    kc = _compress_one(kc_ref, wak_ref, wbk_ref, peak_ref, pebk_ref, w2k_ref, n_chunk)
    ko_ref[0] = _norm_rope(kc, kg_ref[...], ind_ref[...], ra_ref[...], rm_ref[...],
                           rp_ref[...]).astype(jnp.bfloat16)
    vc = _compress_one(vc_ref, wav_ref, wbv_ref, peav_ref, pebv_ref, w2v_ref, n_chunk)
    vto_ref[0] = vc.T.astype(jnp.bfloat16)


def _compress(z, B, S, wk, wv, kg, ind, ra, rm, rp):
    n_chunk = S // N_CHUNK_TOK

    def full(a):
        nd = a.ndim
        return pl.BlockSpec(a.shape, lambda b: (0,) * nd)

    consts = list(wk) + list(wv) + [kg, ind, ra, rm, rp]
    return pl.pallas_call(
        functools.partial(_compress_body, n_chunk=n_chunk),
        out_shape=(
            jax.ShapeDtypeStruct((B, n_chunk, KV_WIDTH), jnp.bfloat16),
            jax.ShapeDtypeStruct((B, KV_WIDTH, n_chunk), jnp.bfloat16),
        ),
        grid=(B,),
        in_specs=[
            pl.BlockSpec((S, KV_WIDTH), lambda b: (b, COL_KC // KV_WIDTH)),
            pl.BlockSpec((S, KV_WIDTH), lambda b: (b, COL_VC // KV_WIDTH)),
        ] + [full(a) for a in consts],
        out_specs=(
            pl.BlockSpec((1, n_chunk, KV_WIDTH), lambda b: (b, 0, 0)),
            pl.BlockSpec((1, KV_WIDTH, n_chunk), lambda b: (b, 0, 0)),
        ),
        compiler_params=_cparams(("arbitrary",)),
        name="compress",
    )(z, z, *consts)


G_COL = HPG * Q_BLOCK
N_COL = N_KV * G_COL


def _group_dots(vts, p):
    p16 = p.astype(jnp.bfloat16)
    return jnp.concatenate(
        [jnp.dot(vt, p16[:, g * G_COL:(g + 1) * G_COL], preferred_element_type=jnp.float32)
         for g, vt in enumerate(vts)], axis=1)


MXU_COLS = 256
ONES_ROWS = 16
ACC_ROWS = HEAD_DIM + ONES_ROWS


def _with_ones(vt):
    return jnp.concatenate([vt, jnp.ones((ONES_ROWS, vt.shape[1]), vt.dtype)], axis=0)


def _tile_softmax(s, vts):
    vones = [_with_ones(vt) for vt in vts]
    ms, accs = [], []
    for c in range(N_COL // MXU_COLS):
        sc = s[:, c * MXU_COLS:(c + 1) * MXU_COLS]
        m = jnp.max(sc, axis=0, keepdims=True)
        p = jnp.exp2(sc - m).astype(jnp.bfloat16)
        ms.append(m)
        accs.append(jnp.dot(vones[c * MXU_COLS // G_COL], p, preferred_element_type=jnp.float32))
    return jnp.concatenate(ms, axis=1), jnp.concatenate(accs, axis=1)


def _merge(state, tile):
    m0, acc0 = state
    m1, acc1 = tile
    m = jnp.maximum(m0, m1)
    return m, jnp.exp2(m0 - m) * acc0 + jnp.exp2(m1 - m) * acc1


def _finish(state):
    _, acc = state
    return acc[0:HEAD_DIM] * (1.0 / acc[HEAD_DIM:HEAD_DIM + 1])


def _nsa_body(qt_ref, kc_ref, vct_ref, ks_ref, vst_ref, kw_ref, vwt_ref, gt_ref, ng0_ref, ng1_ref, e_ref,
              y_ref, ps_ref, bias_ref, sa_ref, sb_ref, *, n_cmp_pad):
    qi = pl.program_id(1)
    q0 = qi * Q_BLOCK
    groups = range(N_KV)
    grows = lambda g: slice(g * HEAD_DIM, (g + 1) * HEAD_DIM)
    gcols = lambda g: slice(g * G_COL, (g + 1) * G_COL)

    qt = qt_ref[0]
    zero = jnp.zeros((HEAD_DIM, G_COL), jnp.bfloat16)
    qrows = []
    for g in groups:
        qcat = jnp.concatenate([qt[(g * HPG + h) * HEAD_DIM:(g * HPG + h + 1) * HEAD_DIM, :]
                                for h in range(HPG)], axis=1)
        qrows.append(jnp.concatenate([qcat if gg == g else zero for gg in groups], axis=1))
    qz = jnp.concatenate(qrows, axis=0)

    tl = lax.broadcasted_iota(jnp.int32, (1, N_COL), 1) % Q_BLOCK
    tcol = q0 + tl

    sc = jnp.dot(kc_ref[0], qz, preferred_element_type=jnp.float32)
    crow = lax.broadcasted_iota(jnp.int32, (n_cmp_pad, N_COL), 0)
    cvalid = crow <= (tcol - (CMP_LEN - 1)) // CMP_STRIDE
    scm = jnp.where(cvalid, sc, NEG)
    mc = jnp.max(scm, axis=0, keepdims=True)
    pc = jnp.exp2(scm - mc)
    lc = jnp.sum(pc, axis=0, keepdims=True)
    pc = pc * jnp.where(tcol >= CMP_LEN - 1, 1.0 / lc, 0.0)
    o_c = _group_dots([vct_ref[0, grows(g), :] for g in groups], pc)

    n_sel = n_cmp_pad // 4
    per = SEL_LEN // CMP_STRIDE
    tq = q0 + lax.broadcasted_iota(jnp.int32, (n_sel, Q_BLOCK), 1)
    bt = tq // SEL_LEN
    rowj = lax.broadcasted_iota(jnp.int32, (n_sel, Q_BLOCK), 0)
    forced = (rowj == 0) | (rowj == bt) | (rowj == bt - 1)
    valid = rowj <= bt
    rowf = rowj.astype(jnp.float32)
    cands = []
    for g in groups:
        psum = pc[:, g * G_COL:g * G_COL + Q_BLOCK]
        for h in range(1, HPG):
            psum = psum + pc[:, g * G_COL + h * Q_BLOCK:g * G_COL + (h + 1) * Q_BLOCK]
        ps_ref[g, 0:SUB, :] = jnp.zeros((SUB, Q_BLOCK), jnp.float32)
        ps_ref[g, SUB:, :] = psum
        strided = lambda k: ps_ref[g, pl.ds(SUB + k, n_sel, stride=per), :]
        imp = strided(0) + strided(1) + strided(2) + 0.5 * strided(3) + 0.5 * strided(-1)
        cands.append(jnp.where(valid & jnp.logical_not(forced), imp, -1.0))

    eligible = [c >= 0.0 for c in cands]
    for _ in range(SEL_TOPN - 3):
        for g in groups:
            mx = jnp.max(cands[g], axis=0, keepdims=True)
            first = jnp.min(jnp.where(cands[g] == mx, rowf, float(n_sel)), axis=0, keepdims=True)
            cands[g] = jnp.where(rowf == first, -1.0, cands[g])
    taken = [(forced & valid) | (eligible[g] & (cands[g] < 0.0)) for g in groups]
    bias_ref[0:n_sel, :] = jnp.concatenate([jnp.where(tk, 0.0, NEG) for tk in taken], axis=1)
    bias_ref[n_sel:, :] = jnp.zeros((LANE, N_KV * Q_BLOCK), jnp.float32)

    win_len = WINDOW + Q_BLOCK
    woff = pl.multiple_of(q0, Q_BLOCK)
    flag = jnp.where(lax.broadcasted_iota(jnp.int32, (LANE, N_COL), 0) == 0, NEG, 0.0)
    sw = jnp.dot(kw_ref[0, pl.ds(woff, win_len), :],
                 jnp.concatenate([qz, flag.astype(jnp.bfloat16)], axis=0),
                 preferred_element_type=jnp.float32)
    wrow = lax.broadcasted_iota(jnp.int32, (Q_BLOCK, N_COL), 0)
    sw = jnp.concatenate([jnp.where(wrow > tl, sw[0:Q_BLOCK], NEG),
                          sw[Q_BLOCK:WINDOW],
                          jnp.where(wrow <= tl, sw[WINDOW:], NEG)], axis=0)
    o_w = _finish(_tile_softmax(sw, [vwt_ref[0, grows(g), pl.ds(woff, win_len)] for g in groups]))

    e_mat = e_ref[...]

    def sel_scores(t):
        koff = pl.multiple_of(t * SEL_TILE, SEL_TILE)
        boff = pl.multiple_of(t * SEL_BLOCKS, SEL_BLOCKS)
        brows = bias_ref[pl.ds(boff, LANE), :].astype(jnp.bfloat16)
        bcols = jnp.concatenate([brows[:, g * Q_BLOCK:(g + 1) * Q_BLOCK] for g in groups for _ in range(HPG)],
                                axis=1)
        rhs = jnp.concatenate([qz, bcols], axis=0)
        lhs = jnp.concatenate([ks_ref[0, pl.ds(koff, SEL_TILE), :], e_mat], axis=1)
        return jnp.dot(lhs, rhs, preferred_element_type=jnp.float32)

    def sel_values(t):
        koff = pl.multiple_of(t * SEL_TILE, SEL_TILE)
        return [vst_ref[0, grows(g), pl.ds(koff, SEL_TILE)] for g in groups]

    def pair(j, state):
        t = 2 * j
        sb_ref[...] = sel_scores(t + 1)
        state = _merge(state, _tile_softmax(sa_ref[...], sel_values(t)))
        sa_ref[...] = sel_scores(t + 2)
        return _merge(state, _tile_softmax(sb_ref[...], sel_values(t + 1)))

    def last_tile(s_ref, t, state):
        srow = lax.broadcasted_iota(jnp.int32, (SEL_TILE, N_COL), 0)
        s = jnp.where(srow <= tcol - t * SEL_TILE, s_ref[...], NEG)
        return _finish(_merge(state, _tile_softmax(s, sel_values(t))))

    n_open = q0 // SEL_TILE
    sa_ref[...] = sel_scores(0)
    state = (jnp.full((1, N_COL), NEG, jnp.float32), jnp.zeros((ACC_ROWS, N_COL), jnp.float32))
    state = lax.fori_loop(0, n_open // 2, pair, state)
    t_next = 2 * (n_open // 2)

    def odd_tail(state):
        sb_ref[...] = sel_scores(t_next + 1)
        state = _merge(state, _tile_softmax(sa_ref[...], sel_values(t_next)))
        return last_tile(sb_ref, t_next + 1, state)

    def even_tail(state):
        return last_tile(sa_ref, t_next, state)

    o_s = lax.cond(n_open % 2 == 1, odd_tail, even_tail, state)

    def gate_row(branch):
        return jnp.concatenate(
            [gt_ref[0, r * N_BRANCH + branch:r * N_BRANCH + branch + 1, :] for r in range(N_KV * HPG)], axis=1)

    yt = gate_row(0) * o_c + gate_row(1) * o_s + gate_row(2) * o_w
    halves = []
    for pr in range(N_KV * HPG // 2):
        two = jnp.concatenate([yt[:, (2 * pr) * Q_BLOCK:(2 * pr + 1) * Q_BLOCK],
                               yt[:, (2 * pr + 1) * Q_BLOCK:(2 * pr + 2) * Q_BLOCK]], axis=0)
        halves.append(two.T)
    y = jnp.concatenate(halves, axis=1)
    ng = jnp.concatenate([ng0_ref[...], ng1_ref[...]], axis=1)
    y_ref[...] = (y * (ng * jax.nn.sigmoid(ng))).astype(y_ref.dtype)


def _nsa(z, qt, kc, vct, ks, vst, kw, vwt, gt, B, S):
    nq = S // Q_BLOCK
    n_cmp_pad = S // CMP_STRIDE
    gw = HPG * HEAD_DIM
    bf = jnp.bfloat16
    e_mat = (jnp.arange(SEL_TILE)[:, None] // SEL_LEN == jnp.arange(LANE)[None, :]).astype(bf)
    lead = jnp.zeros((B, WINDOW, 2 * KV_WIDTH), bf).at[:, :, KV_WIDTH].set(1.0)
    kw_pad = jnp.concatenate([lead, jnp.concatenate([kw, jnp.zeros_like(kw)], axis=2)], axis=1)
    vwt_pad = jnp.pad(vwt, ((0, 0), (0, 0), (WINDOW, 0)))
    whole = lambda a: pl.BlockSpec((1,) + a.shape[1:], lambda b, i: (b, 0, 0))
    return pl.pallas_call(
        functools.partial(_nsa_body, n_cmp_pad=n_cmp_pad),
        out_shape=jax.ShapeDtypeStruct((B * S, NSA_WIDTH), bf),
        grid=(B, nq),
        in_specs=[
            pl.BlockSpec((1, NSA_WIDTH, Q_BLOCK), lambda b, i: (b, 0, i)),
            whole(kc), whole(vct), whole(ks), whole(vst), whole(kw_pad), whole(vwt_pad),
            pl.BlockSpec((1, LANE, Q_BLOCK), lambda b, i: (b, 0, i)),
            pl.BlockSpec((Q_BLOCK, gw), lambda b, i: (b * nq + i, COL_NG // gw)),
            pl.BlockSpec((Q_BLOCK, gw), lambda b, i: (b * nq + i, COL_NG // gw + 1)),
            pl.BlockSpec((SEL_TILE, LANE), lambda b, i: (0, 0)),
        ],
        out_specs=pl.BlockSpec((Q_BLOCK, NSA_WIDTH), lambda b, i: (b * nq + i, 0)),
        scratch_shapes=[
            pltpu.VMEM((N_KV, SUB + n_cmp_pad, Q_BLOCK), jnp.float32),
            pltpu.VMEM((S // SEL_LEN + LANE, N_KV * Q_BLOCK), jnp.float32),
            pltpu.VMEM((SEL_TILE, N_COL), jnp.float32),
            pltpu.VMEM((SEL_TILE, N_COL), jnp.float32),
        ],
        compiler_params=_cparams(("arbitrary", "arbitrary")),
        name="nsa",
    )(qt, kc, vct, ks, vst, kw_pad, vwt_pad, gt, z, z, e_mat)


def _out_proj_body(x_ref, ya_ref, yb_ref, wa_ref, wb_ref, o_ref):
    o_ref[...] = (x_ref[...]
                  + jnp.dot(ya_ref[...], wa_ref[...], preferred_element_type=jnp.float32)
                  + jnp.dot(yb_ref[...], wb_ref[...], preferred_element_type=jnp.float32))


def _out_proj(x2, ya, yb, wa, wb):
    rows = x2.shape[0]
    return pl.pallas_call(
        _out_proj_body,
        out_shape=jax.ShapeDtypeStruct((rows, D_MODEL), jnp.float32),
        grid=(rows // ROW_TILE,),
        in_specs=[
            pl.BlockSpec((ROW_TILE, D_MODEL), lambda i: (i, 0)),
            pl.BlockSpec((ROW_TILE, RG_WIDTH), lambda i: (i, 0)),
            pl.BlockSpec((ROW_TILE, NSA_WIDTH), lambda i: (i, 0)),
            pl.BlockSpec((RG_WIDTH, D_MODEL), lambda i: (0, 0)),
            pl.BlockSpec((NSA_WIDTH, D_MODEL), lambda i: (0, 0)),
        ],
        out_specs=pl.BlockSpec((ROW_TILE, D_MODEL), lambda i: (i, 0)),
        compiler_params=_cparams(("arbitrary",)),
        name="out_proj",
    )(x2, ya, yb, wa, wb)


def _rope_tables(pos):
    half = ROPE_DIM // 2
    inv = ROPE_THETA ** (-jnp.arange(half, dtype=jnp.float32) / half)
    ang = pos.astype(jnp.float32)[:, None] * inv[None, :]
    cos, sin = jnp.cos(ang), jnp.sin(ang)
    n = pos.shape[0]
    rest = HEAD_DIM - ROPE_DIM
    ra = jnp.concatenate([cos, cos, jnp.ones((n, rest), jnp.float32)], axis=1)
    rm = jnp.concatenate([-sin, jnp.zeros((n, half + rest), jnp.float32)], axis=1)
    rp = jnp.concatenate([jnp.zeros((n, half), jnp.float32), sin, jnp.zeros((n, rest), jnp.float32)], axis=1)
    return tuple(jnp.tile(t, (1, LANE // HEAD_DIM)) for t in (ra, rm, rp))


def _block_diag(blocks):
    n, r, c = blocks.shape
    eye = jnp.eye(n, dtype=blocks.dtype)
    return (eye[:, None, :, None] * blocks[:, :, None, :]).reshape(n * r, n * c)


def _compress_params(pe, w1, w2):
    bf = jnp.bfloat16
    w1t = w1.reshape(2, N_CHUNK_TOK, HEAD_DIM, CMP_HIDDEN)
    dup = lambda w: jax.vmap(lambda m: _block_diag(jnp.stack([m] * N_KV)))(w)
    wa, wb = dup(w1t[0]).astype(bf), dup(w1t[1]).astype(bf)
    pet = jnp.tile(pe.reshape(2, N_CHUNK_TOK, 1, HEAD_DIM), (1, 1, 1, N_KV))
    w2d = _block_diag(jnp.stack([w2] * N_KV)).astype(bf)
    return wa, wb, pet[0], pet[1], w2d


def _layer(x2, B, S, norm_g, w_in, conv_w, conv_b, rg_wr, rg_br, rg_wi, rg_bi, rg_lam,
           q_g, k_g, pe_k, w1_k, w2_k, pe_v, w1_v, w2_v, w_out, tabs_tok, tabs_cmp, ind):
    bf = jnp.bfloat16
    w_in_p = jnp.pad(w_in, ((0, 0), (0, N_PAD - N_IN))).astype(bf)
    z = _in_proj(x2, norm_g[None, :], w_in_p)

    y_a = _rglru(z, B, S, conv_w, conv_b[None, :], _block_diag(rg_wr).astype(bf), rg_br[None, :],
                 _block_diag(rg_wi).astype(bf), rg_bi[None, :], rg_lam[None, :])

    tile2 = lambda v: jnp.tile(v, LANE // HEAD_DIM)[None, :]
    qt, ks, vst, kw, vwt, gt = _prep(z, B, S, *tabs_tok, tile2(q_g), tile2(k_g[1]), tile2(k_g[2]), ind)
    kc, vct = _compress(z, B, S, _compress_params(pe_k, w1_k, w2_k), _compress_params(pe_v, w1_v, w2_v),
                        tile2(k_g[0]), ind, *tabs_cmp)
    y_b = _nsa(z, qt, kc, vct, ks, vst, kw, vwt, gt, B, S)

    w_out16 = w_out.astype(bf)
    return _out_proj(x2, y_a, y_b, w_out16[:RG_WIDTH], w_out16[RG_WIDTH:])


def kernel(x, norm_g, w_in, conv_w, conv_b, rg_wr, rg_br, rg_wi, rg_bi, rg_lambda, q_norm_g, k_norm_g,
           cmp_pe_k, cmp_w1_k, cmp_w2_k, cmp_pe_v, cmp_w1_v, cmp_w2_v, w_out):
    B, S, D = x.shape
    assert D == D_MODEL and S % ROW_TILE == 0 and S % SEL_TILE == 0 and S >= WINDOW
    depth = norm_g.shape[0]
    tabs_tok = _rope_tables(jnp.arange(S))
    tabs_cmp = _rope_tables(jnp.arange(S // CMP_STRIDE) * CMP_STRIDE + (CMP_LEN - 1))
    ind = _block_diag(jnp.ones((LANE // HEAD_DIM, HEAD_DIM, HEAD_DIM), jnp.bfloat16))
    x2 = x.reshape(B * S, D)
    for l in range(depth):
        x2 = _layer(x2, B, S, norm_g[l], w_in[l], conv_w[l], conv_b[l], rg_wr[l], rg_br[l], rg_wi[l],
                    rg_bi[l], rg_lambda[l], q_norm_g[l], k_norm_g[l], cmp_pe_k[l], cmp_w1_k[l],
                    cmp_w2_k[l], cmp_pe_v[l], cmp_w1_v[l], cmp_w2_v[l], w_out[l], tabs_tok, tabs_cmp, ind)
    return x2.reshape(B, S, D)
```

```python
import functools
import math

import numpy as np
import jax
import jax.numpy as jnp
from jax import lax
from jax.experimental import pallas as pl
from jax.experimental.pallas import tpu as pltpu

D_MODEL = 1024
RG_WIDTH = 512
RG_BLOCKS = 8
RG_BLOCK = 64
CONV_W = 4
RG_C = 8.0
NSA_WIDTH = 512
HEAD_DIM = 64
N_KV = 2
HPG = 4
KV_WIDTH = N_KV * HEAD_DIM
ROPE_DIM = 16
ROPE_THETA = 500000.0
CMP_LEN = 32
CMP_STRIDE = 16
CMP_HIDDEN = 128
SEL_LEN = 64
SEL_TOPN = 16
WINDOW = 512
N_BRANCH = 3
Q_BLOCK = 128
EPS = 1e-6
NEG = -1e30
N_IN = 2 * RG_WIDTH + 2 * NSA_WIDTH + 6 * KV_WIDTH + N_BRANCH * 8
LANE = 128
N_PAD = ((N_IN + LANE - 1) // LANE) * LANE

COL_Q = 2 * RG_WIDTH
COL_KC = COL_Q + NSA_WIDTH
COL_VC = COL_KC + KV_WIDTH
COL_KS = COL_VC + KV_WIDTH
COL_VS = COL_KS + KV_WIDTH
COL_KW = COL_VS + KV_WIDTH
COL_VW = COL_KW + KV_WIDTH
COL_NG = COL_VW + KV_WIDTH
COL_BG = COL_NG + NSA_WIDTH

VMEM_LIMIT = 56 * 1024 * 1024

ROW_TILE = 512
SEL_TILE = 512
SEL_BLOCKS = SEL_TILE // SEL_LEN
WIN_TILE = 128
LOG2E = 1.4426950408889634


def _cparams(sem):
    return pltpu.CompilerParams(dimension_semantics=sem, vmem_limit_bytes=VMEM_LIMIT)


def _head_mean_sq(x, ind):
    sq = x * x
    hi = sq.astype(jnp.bfloat16)
    lo = (sq - hi.astype(jnp.float32)).astype(jnp.bfloat16)
    tot = (jnp.dot(hi, ind, preferred_element_type=jnp.float32)
           + jnp.dot(lo, ind, preferred_element_type=jnp.float32))
    return tot * (1.0 / HEAD_DIM)


def _norm_rope(x, gain, ind, ra, rm, rp):
    width = x.shape[-1]
    y = (x * lax.rsqrt(_head_mean_sq(x, ind) + EPS)) * gain
    up = pltpu.roll(y, width - ROPE_DIM // 2, axis=1)
    dn = pltpu.roll(y, ROPE_DIM // 2, axis=1)
    return y * ra + up * rm + dn * rp


def _prep_q(zq, tabs, qg_ref, ind_ref, qt_ref):
    scale = LOG2E / math.sqrt(HEAD_DIM)
    for c in range(NSA_WIDTH // LANE):
        sl = slice(c * LANE, (c + 1) * LANE)
        qn = _norm_rope(zq[:, sl], qg_ref[...], ind_ref[...], *tabs) * scale
        qt_ref[0, sl, :] = qn.T.astype(jnp.bfloat16)


def _prep_kv(zkv, zbg, tabs, kgs_ref, kgw_ref, ind_ref, kso_ref, vst_ref, kwo_ref, vwt_ref, gt_ref):
    kv = [zkv[:, c * KV_WIDTH:(c + 1) * KV_WIDTH] for c in range(4)]
    kso_ref[0] = _norm_rope(kv[0], kgs_ref[...], ind_ref[...], *tabs).astype(jnp.bfloat16)
    kwo_ref[0] = _norm_rope(kv[2], kgw_ref[...], ind_ref[...], *tabs).astype(jnp.bfloat16)
    vst_ref[0] = kv[1].T.astype(jnp.bfloat16)
    vwt_ref[0] = kv[3].T.astype(jnp.bfloat16)
    gt_ref[0] = jax.nn.sigmoid(zbg).T


RG_TILE = 512
SUB = 8


def _rglru_tile(x, gate, cw_ref, cb_ref, wr_ref, br_ref, wi_ref, bi_ref, lam_ref,
                y_ref, tail_ref, carry_ref, a_ref, b_ref):
    tail = tail_ref[...]
    row8 = lax.broadcasted_iota(jnp.int32, (SUB, RG_WIDTH), 0)
    conv = x * cw_ref[CONV_W - 1:CONV_W, :] + cb_ref[...]
    for s in range(1, CONV_W):
        xs = pltpu.roll(x, s, axis=0)
        first = jnp.where(row8 < s, pltpu.roll(tail, s, axis=0), xs[0:SUB])
        xs = jnp.concatenate([first, xs[SUB:]], axis=0)
        conv = conv + xs * cw_ref[CONV_W - 1 - s:CONV_W - s, :]
    tail_ref[...] = x[RG_TILE - SUB:]
    yield

    cb16 = conv.astype(jnp.bfloat16)
    r = jax.nn.sigmoid(jnp.dot(cb16, wr_ref[...], preferred_element_type=jnp.float32) + br_ref[...])
    ig = jax.nn.sigmoid(jnp.dot(cb16, wi_ref[...], preferred_element_type=jnp.float32) + bi_ref[...])
    log_a = (-RG_C * r) * jax.nn.softplus(-lam_ref[...])
    a = jnp.exp(log_a)
    u = jnp.sqrt(-jnp.tanh(log_a) * (a * a + 1.0)) * (ig * conv)
    yield

    grp = (RG_TILE // SUB, SUB, RG_WIDTH)
    a, u = a.reshape(grp), u.reshape(grp)
    row = lax.broadcasted_iota(jnp.int32, grp, 1)
    for d in (1, 2, 4):
        keep = row >= d
        a_s = jnp.where(keep, pltpu.roll(a, d, axis=1), 1.0)
        u_s = jnp.where(keep, pltpu.roll(u, d, axis=1), 0.0)
        u = a * u_s + u
        a = a * a_s
    a_ref[...] = a.reshape(RG_TILE, RG_WIDTH)
    b_ref[...] = u.reshape(RG_TILE, RG_WIDTH)
    yield

    def step(k, h):
        off = pl.multiple_of(k * SUB, SUB)
        hk = a_ref[pl.ds(off, SUB), :] * h + b_ref[pl.ds(off, SUB), :]
        b_ref[pl.ds(off, SUB), :] = hk
        return jnp.broadcast_to(hk[SUB - 1:SUB, :], (SUB, RG_WIDTH))

    carry_ref[...] = lax.fori_loop(0, RG_TILE // SUB, step, carry_ref[...], unroll=True)
    yield
    y_ref[...] = (b_ref[...] * (gate * jax.nn.sigmoid(gate))).astype(y_ref.dtype)
    yield


def _front_body(x_ref, g_ref, w_ref,
                cw_ref, cb_ref, wr_ref, br_ref, wi_ref, bi_ref, lam_ref,
                ra_ref, rm_ref, rp_ref, qg_ref, kgs_ref, kgw_ref, ind_ref,
                ya_ref, qt_ref, kso_ref, vst_ref, kwo_ref, vwt_ref, gt_ref, kcvc_ref, ng_ref,
                tail_ref, carry_ref, a_ref, b_ref):
    @pl.when(pl.program_id(1) == 0)
    def _():
        tail_ref[...] = jnp.zeros_like(tail_ref)
        carry_ref[...] = jnp.zeros_like(carry_ref)

    x = x_ref[...]
    ms = jnp.mean(x * x, axis=-1, keepdims=True)
    h = ((x * lax.rsqrt(ms + EPS)) * g_ref[...]).astype(jnp.bfloat16)
    proj = lambda c0, c1: jnp.dot(h, w_ref[:, c0:c1], preferred_element_type=jnp.float32)

    z_rg = proj(0, COL_Q)
    mixer = _rglru_tile(z_rg[:, :RG_WIDTH], z_rg[:, RG_WIDTH:], cw_ref, cb_ref, wr_ref, br_ref, wi_ref,
                        bi_ref, lam_ref, ya_ref, tail_ref, carry_ref, a_ref, b_ref)
    tabs = (ra_ref[...], rm_ref[...], rp_ref[...])
    next(mixer)
    zq = proj(COL_Q, COL_KC)
    next(mixer)
    zkv = proj(COL_KS, COL_NG)
    kcvc_ref[...] = proj(COL_KC, COL_KS)
    next(mixer)
    ng_ref[...] = proj(COL_NG, COL_BG)
    zbg = proj(COL_BG, N_PAD)
    next(mixer)
    _prep_q(zq, tabs, qg_ref, ind_ref, qt_ref)
    next(mixer)
    _prep_kv(zkv, zbg, tabs, kgs_ref, kgw_ref, ind_ref, kso_ref, vst_ref, kwo_ref, vwt_ref, gt_ref)


def _front(x2, B, S, g, w, cw, cb, wr, br, wi, bi, lam, ra, rm, rp, qg, kgs, kgw, ind):
    nt = S // ROW_TILE
    bf = jnp.bfloat16
    rows = lambda width: pl.BlockSpec((ROW_TILE, width), lambda b, i: (b * nt + i, 0))
    const = lambda a: pl.BlockSpec(a.shape, lambda b, i: (0, 0))
    tab = pl.BlockSpec((ROW_TILE, LANE), lambda b, i: (i, 0))
    tok = lambda width: pl.BlockSpec((1, ROW_TILE, width), lambda b, i: (b, i, 0))
    tr = lambda height: pl.BlockSpec((1, height, ROW_TILE), lambda b, i: (b, 0, i))
    return pl.pallas_call(
        _front_body,
        out_shape=(
            jax.ShapeDtypeStruct((B * S, RG_WIDTH), bf),
            jax.ShapeDtypeStruct((B, NSA_WIDTH, S), bf),
            jax.ShapeDtypeStruct((B, S, KV_WIDTH), bf),
            jax.ShapeDtypeStruct((B, KV_WIDTH, S), bf),
            jax.ShapeDtypeStruct((B, S, KV_WIDTH), bf),
            jax.ShapeDtypeStruct((B, KV_WIDTH, S), bf),
            jax.ShapeDtypeStruct((B, LANE, S), jnp.float32),
            jax.ShapeDtypeStruct((B * S, 2 * KV_WIDTH), jnp.float32),
            jax.ShapeDtypeStruct((B * S, NSA_WIDTH), jnp.float32),
        ),
        grid=(B, nt),
        in_specs=[rows(D_MODEL), const(g), const(w),
                  const(cw), const(cb), const(wr), const(br), const(wi), const(bi), const(lam),
                  tab, tab, tab, const(qg), const(kgs), const(kgw), const(ind)],
        out_specs=(rows(RG_WIDTH), tr(NSA_WIDTH), tok(KV_WIDTH), tr(KV_WIDTH), tok(KV_WIDTH), tr(KV_WIDTH),
                   tr(LANE), rows(2 * KV_WIDTH), rows(NSA_WIDTH)),
        scratch_shapes=[
            pltpu.VMEM((SUB, RG_WIDTH), jnp.float32),
            pltpu.VMEM((SUB, RG_WIDTH), jnp.float32),
            pltpu.VMEM((RG_TILE, RG_WIDTH), jnp.float32),
            pltpu.VMEM((RG_TILE, RG_WIDTH), jnp.float32),
        ],
        compiler_params=_cparams(("arbitrary", "arbitrary")),
        name="front",
    )(x2, g, w, cw, cb, wr, br, wi, bi, lam, ra, rm, rp, qg, kgs, kgw, ind)


N_CHUNK_TOK = CMP_STRIDE


def _compress_one(src_ref, wa_ref, wb_ref, pea_ref, peb_ref, w2_ref, n_chunk):
    ha = jnp.zeros((n_chunk, 2 * CMP_HIDDEN), jnp.float32)
    hb = jnp.zeros((n_chunk, 2 * CMP_HIDDEN), jnp.float32)
    for j in range(N_CHUNK_TOK):
        xj = src_ref[pl.ds(j, n_chunk, stride=N_CHUNK_TOK), :]
        ha = ha + jnp.dot((xj + pea_ref[j]).astype(jnp.bfloat16), wa_ref[j],
                          preferred_element_type=jnp.float32)
        hb = hb + jnp.dot((xj + peb_ref[j]).astype(jnp.bfloat16), wb_ref[j],
                          preferred_element_type=jnp.float32)
    hid = ha + pltpu.roll(hb, n_chunk - 1, axis=0)
    act = hid * jax.nn.sigmoid(hid)
    return jnp.dot(act.astype(jnp.bfloat16), w2_ref[...], preferred_element_type=jnp.float32)


def _compress_body(kc_ref, vc_ref, wak_ref, wbk_ref, peak_ref, pebk_ref, w2k_ref,
                   wav_ref, wbv_ref, peav_ref, pebv_ref, w2v_ref,
                   kg_ref, ind_ref, ra_ref, rm_ref, rp_ref, ko_ref, vto_ref, *, n_chunk):
    kc = _compress_one(kc_ref, wak_ref, wbk_ref, peak_ref, pebk_ref, w2k_ref, n_chunk)
    ko_ref[0] = _norm_rope(kc, kg_ref[...], ind_ref[...], ra_ref[...], rm_ref[...],
                           rp_ref[...]).astype(jnp.bfloat16)
    vc = _compress_one(vc_ref, wav_ref, wbv_ref, peav_ref, pebv_ref, w2v_ref, n_chunk)
    vto_ref[0] = vc.T.astype(jnp.bfloat16)


def _compress(z, B, S, wk, wv, kg, ind, ra, rm, rp):
    n_chunk = S // N_CHUNK_TOK

    def full(a):
        nd = a.ndim
        return pl.BlockSpec(a.shape, lambda b: (0,) * nd)

    consts = list(wk) + list(wv) + [kg, ind, ra, rm, rp]
    return pl.pallas_call(
        functools.partial(_compress_body, n_chunk=n_chunk),
        out_shape=(
            jax.ShapeDtypeStruct((B, n_chunk, KV_WIDTH), jnp.bfloat16),
            jax.ShapeDtypeStruct((B, KV_WIDTH, n_chunk), jnp.bfloat16),
        ),
        grid=(B,),
        in_specs=[
            pl.BlockSpec((S, KV_WIDTH), lambda b: (b, 0)),
            pl.BlockSpec((S, KV_WIDTH), lambda b: (b, 1)),
        ] + [full(a) for a in consts],
        out_specs=(
            pl.BlockSpec((1, n_chunk, KV_WIDTH), lambda b: (b, 0, 0)),
            pl.BlockSpec((1, KV_WIDTH, n_chunk), lambda b: (b, 0, 0)),
        ),
        compiler_params=_cparams(("arbitrary",)),
        name="compress",
    )(z, z, *consts)


G_COL = HPG * Q_BLOCK
N_COL = N_KV * G_COL


def _group_dots(vts, p):
    p16 = p.astype(jnp.bfloat16)
    return jnp.concatenate(
        [jnp.dot(vt, p16[:, g * G_COL:(g + 1) * G_COL], preferred_element_type=jnp.float32)
         for g, vt in enumerate(vts)], axis=1)


MXU_COLS = 256
ONES_ROWS = 16
ACC_ROWS = HEAD_DIM + ONES_ROWS


def _with_ones(vt):
    return jnp.concatenate([vt, jnp.ones((ONES_ROWS, vt.shape[1]), vt.dtype)], axis=0)


def _tile_softmax(s, vts):
    vones = [_with_ones(vt) for vt in vts]
    ms, accs = [], []
    for c in range(N_COL // MXU_COLS):
        sc = s[:, c * MXU_COLS:(c + 1) * MXU_COLS]
        m = jnp.max(sc, axis=0, keepdims=True)
        p = jnp.exp2(sc - m).astype(jnp.bfloat16)
        ms.append(m)
        accs.append(jnp.dot(vones[c * MXU_COLS // G_COL], p, preferred_element_type=jnp.float32))
    return jnp.concatenate(ms, axis=1), jnp.concatenate(accs, axis=1)


def _merge(state, tile):
    m0, acc0 = state
    m1, acc1 = tile
    m = jnp.maximum(m0, m1)
    return m, jnp.exp2(m0 - m) * acc0 + jnp.exp2(m1 - m) * acc1


def _finish(state):
    _, acc = state
    return acc[0:HEAD_DIM] * (1.0 / acc[HEAD_DIM:HEAD_DIM + 1])


def _nsa_body(qt_ref, kc_ref, vct_ref, ks_ref, vst_ref, kw_ref, vwt_ref, gt_ref, ng0_ref, ng1_ref, e_ref,
              y_ref, ps_ref, bias_ref, sa_ref, sb_ref, *, n_cmp_pad):
    qi = pl.program_id(1)
    q0 = qi * Q_BLOCK
    groups = range(N_KV)
    grows = lambda g: slice(g * HEAD_DIM, (g + 1) * HEAD_DIM)
    gcols = lambda g: slice(g * G_COL, (g + 1) * G_COL)

    qt = qt_ref[0]
    zero = jnp.zeros((HEAD_DIM, G_COL), jnp.bfloat16)
    qrows = []
    for g in groups:
        qcat = jnp.concatenate([qt[(g * HPG + h) * HEAD_DIM:(g * HPG + h + 1) * HEAD_DIM, :]
                                for h in range(HPG)], axis=1)
        qrows.append(jnp.concatenate([qcat if gg == g else zero for gg in groups], axis=1))
    qz = jnp.concatenate(qrows, axis=0)

    tl = lax.broadcasted_iota(jnp.int32, (1, N_COL), 1) % Q_BLOCK
    tcol = q0 + tl

    per = SEL_LEN // CMP_STRIDE
    n_sel_all = n_cmp_pad // per

    def compressed_and_topk(n_cmp):
        sc = jnp.dot(kc_ref[0, 0:n_cmp, :], qz, preferred_element_type=jnp.float32)
        crow = lax.broadcasted_iota(jnp.int32, (n_cmp, N_COL), 0)
        cvalid = crow <= (tcol - (CMP_LEN - 1)) // CMP_STRIDE
        scm = jnp.where(cvalid, sc, NEG)
        mc = jnp.max(scm, axis=0, keepdims=True)
        pc = jnp.exp2(scm - mc)
        lc = jnp.sum(pc, axis=0, keepdims=True)
        pc = pc * jnp.where(tcol >= CMP_LEN - 1, 1.0 / lc, 0.0)
        o_c = _group_dots([vct_ref[0, grows(g), 0:n_cmp] for g in groups], pc)

        n_sel = n_cmp // per
        tq = q0 + lax.broadcasted_iota(jnp.int32, (n_sel, Q_BLOCK), 1)
        bt = tq // SEL_LEN
        rowj = lax.broadcasted_iota(jnp.int32, (n_sel, Q_BLOCK), 0)
        forced = (rowj == 0) | (rowj == bt) | (rowj == bt - 1)
        valid = rowj <= bt
        rowf = rowj.astype(jnp.float32)
        cands = []
        for g in groups:
            psum = pc[:, g * G_COL:g * G_COL + Q_BLOCK]
            for h in range(1, HPG):
                psum = psum + pc[:, g * G_COL + h * Q_BLOCK:g * G_COL + (h + 1) * Q_BLOCK]
            ps_ref[g, 0:SUB, :] = jnp.zeros((SUB, Q_BLOCK), jnp.float32)
            ps_ref[g, SUB:SUB + n_cmp, :] = psum
            strided = lambda k: ps_ref[g, pl.ds(SUB + k, n_sel, stride=per), :]
            imp = strided(0) + strided(1) + strided(2) + 0.5 * strided(3) + 0.5 * strided(-1)
            cands.append(jnp.where(valid & jnp.logical_not(forced), imp, -1.0))

        eligible = [c >= 0.0 for c in cands]
        for _ in range(SEL_TOPN - 3):
            for g in groups:
                mx = jnp.max(cands[g], axis=0, keepdims=True)
                first = jnp.min(jnp.where(cands[g] == mx, rowf, float(n_sel)), axis=0, keepdims=True)
                cands[g] = jnp.where(rowf == first, -1.0, cands[g])
        taken = [(forced & valid) | (eligible[g] & (cands[g] < 0.0)) for g in groups]
        bias_ref[0:n_sel, :] = jnp.concatenate([jnp.where(tk, 0.0, NEG) for tk in taken], axis=1)
        if n_sel < n_sel_all:
            bias_ref[n_sel:n_sel_all, :] = jnp.full((n_sel_all - n_sel, N_KV * Q_BLOCK), NEG, jnp.float32)
        bias_ref[n_sel_all:, :] = jnp.zeros((LANE, N_KV * Q_BLOCK), jnp.float32)
        return o_c

    n_var = 4
    quarter = n_cmp_pad // n_var
    variant = jnp.minimum((q0 + Q_BLOCK - CMP_LEN) // CMP_STRIDE // quarter, n_var - 1)
    o_c = lax.switch(variant, [functools.partial(compressed_and_topk, (v + 1) * quarter) for v in range(n_var)])

    win_len = WINDOW + Q_BLOCK
    woff = pl.multiple_of(q0, Q_BLOCK)
    flag = jnp.where(lax.broadcasted_iota(jnp.int32, (LANE, N_COL), 0) == 0, NEG, 0.0)
    sw = jnp.dot(kw_ref[0, pl.ds(woff, win_len), :],
                 jnp.concatenate([qz, flag.astype(jnp.bfloat16)], axis=0),
                 preferred_element_type=jnp.float32)
    wrow = lax.broadcasted_iota(jnp.int32, (Q_BLOCK, N_COL), 0)
    sw = jnp.concatenate([jnp.where(wrow > tl, sw[0:Q_BLOCK], NEG),
                          sw[Q_BLOCK:WINDOW],
                          jnp.where(wrow <= tl, sw[WINDOW:], NEG)], axis=0)
    o_w = _finish(_tile_softmax(sw, [vwt_ref[0, grows(g), pl.ds(woff, win_len)] for g in groups]))

    e_mat = e_ref[...]

    def sel_scores(t):
        koff = pl.multiple_of(t * SEL_TILE, SEL_TILE)
        boff = pl.multiple_of(t * SEL_BLOCKS, SEL_BLOCKS)
        brows = bias_ref[pl.ds(boff, LANE), :].astype(jnp.bfloat16)
        bcols = jnp.concatenate([brows[:, g * Q_BLOCK:(g + 1) * Q_BLOCK] for g in groups for _ in range(HPG)],
                                axis=1)
        rhs = jnp.concatenate([qz, bcols], axis=0)
        lhs = jnp.concatenate([ks_ref[0, pl.ds(koff, SEL_TILE), :], e_mat], axis=1)
        return jnp.dot(lhs, rhs, preferred_element_type=jnp.float32)

    def sel_values(t):
        koff = pl.multiple_of(t * SEL_TILE, SEL_TILE)
        return [vst_ref[0, grows(g), pl.ds(koff, SEL_TILE)] for g in groups]

    def pair(j, state):
        t = 2 * j
        sb_ref[...] = sel_scores(t + 1)
        state = _merge(state, _tile_softmax(sa_ref[...], sel_values(t)))
        sa_ref[...] = sel_scores(t + 2)
        return _merge(state, _tile_softmax(sb_ref[...], sel_values(t + 1)))

    def last_tile(s_ref, t, state):
        srow = lax.broadcasted_iota(jnp.int32, (SEL_TILE, N_COL), 0)
        s = jnp.where(srow <= tcol - t * SEL_TILE, s_ref[...], NEG)
        return _finish(_merge(state, _tile_softmax(s, sel_values(t))))

    n_open = q0 // SEL_TILE
    sa_ref[...] = sel_scores(0)
    state = (jnp.full((1, N_COL), NEG, jnp.float32), jnp.zeros((ACC_ROWS, N_COL), jnp.float32))
    state = lax.fori_loop(0, n_open // 2, pair, state)
    t_next = 2 * (n_open // 2)

    def odd_tail(state):
        sb_ref[...] = sel_scores(t_next + 1)
        state = _merge(state, _tile_softmax(sa_ref[...], sel_values(t_next)))
        return last_tile(sb_ref, t_next + 1, state)

    def even_tail(state):
        return last_tile(sa_ref, t_next, state)

    o_s = lax.cond(n_open % 2 == 1, odd_tail, even_tail, state)

    def gate_row(branch):
        return jnp.concatenate(
            [gt_ref[0, r * N_BRANCH + branch:r * N_BRANCH + branch + 1, :] for r in range(N_KV * HPG)], axis=1)

    yt = gate_row(0) * o_c + gate_row(1) * o_s + gate_row(2) * o_w
    halves = []
    for pr in range(N_KV * HPG // 2):
        two = jnp.concatenate([yt[:, (2 * pr) * Q_BLOCK:(2 * pr + 1) * Q_BLOCK],
                               yt[:, (2 * pr + 1) * Q_BLOCK:(2 * pr + 2) * Q_BLOCK]], axis=0)
        halves.append(two.T)
    y = jnp.concatenate(halves, axis=1)
    ng = jnp.concatenate([ng0_ref[...], ng1_ref[...]], axis=1)
    y_ref[...] = (y * (ng * jax.nn.sigmoid(ng))).astype(y_ref.dtype)


def _nsa(z, qt, kc, vct, ks, vst, kw, vwt, gt, B, S):
    nq = S // Q_BLOCK
    n_cmp_pad = S // CMP_STRIDE
    gw = HPG * HEAD_DIM
    bf = jnp.bfloat16
    e_mat = (jnp.arange(SEL_TILE)[:, None] // SEL_LEN == jnp.arange(LANE)[None, :]).astype(bf)
    lead = jnp.zeros((B, WINDOW, 2 * KV_WIDTH), bf).at[:, :, KV_WIDTH].set(1.0)
    kw_pad = jnp.concatenate([lead, jnp.concatenate([kw, jnp.zeros_like(kw)], axis=2)], axis=1)
    vwt_pad = jnp.pad(vwt, ((0, 0), (0, 0), (WINDOW, 0)))
    whole = lambda a: pl.BlockSpec((1,) + a.shape[1:], lambda b, i: (b, 0, 0))
    return pl.pallas_call(
        functools.partial(_nsa_body, n_cmp_pad=n_cmp_pad),
        out_shape=jax.ShapeDtypeStruct((B * S, NSA_WIDTH), bf),
        grid=(B, nq),
        in_specs=[
            pl.BlockSpec((1, NSA_WIDTH, Q_BLOCK), lambda b, i: (b, 0, i)),
            whole(kc), whole(vct), whole(ks), whole(vst), whole(kw_pad), whole(vwt_pad),
            pl.BlockSpec((1, LANE, Q_BLOCK), lambda b, i: (b, 0, i)),
            pl.BlockSpec((Q_BLOCK, gw), lambda b, i: (b * nq + i, 0)),
            pl.BlockSpec((Q_BLOCK, gw), lambda b, i: (b * nq + i, 1)),
            pl.BlockSpec((SEL_TILE, LANE), lambda b, i: (0, 0)),
        ],
        out_specs=pl.BlockSpec((Q_BLOCK, NSA_WIDTH), lambda b, i: (b * nq + i, 0)),
        scratch_shapes=[
            pltpu.VMEM((N_KV, SUB + n_cmp_pad, Q_BLOCK), jnp.float32),
            pltpu.VMEM((S // SEL_LEN + LANE, N_KV * Q_BLOCK), jnp.float32),
            pltpu.VMEM((SEL_TILE, N_COL), jnp.float32),
            pltpu.VMEM((SEL_TILE, N_COL), jnp.float32),
        ],
        compiler_params=_cparams(("arbitrary", "arbitrary")),
        name="nsa",
    )(qt, kc, vct, ks, vst, kw_pad, vwt_pad, gt, z, z, e_mat)


def _out_proj_body(x_ref, ya_ref, yb_ref, wa_ref, wb_ref, o_ref):
    o_ref[...] = (x_ref[...]
                  + jnp.dot(ya_ref[...], wa_ref[...], preferred_element_type=jnp.float32)
                  + jnp.dot(yb_ref[...], wb_ref[...], preferred_element_type=jnp.float32))


def _out_proj(x2, ya, yb, wa, wb):
    rows = x2.shape[0]
    return pl.pallas_call(
        _out_proj_body,
        out_shape=jax.ShapeDtypeStruct((rows, D_MODEL), jnp.float32),
        grid=(rows // ROW_TILE,),
        in_specs=[
            pl.BlockSpec((ROW_TILE, D_MODEL), lambda i: (i, 0)),
            pl.BlockSpec((ROW_TILE, RG_WIDTH), lambda i: (i, 0)),
            pl.BlockSpec((ROW_TILE, NSA_WIDTH), lambda i: (i, 0)),
            pl.BlockSpec((RG_WIDTH, D_MODEL), lambda i: (0, 0)),
            pl.BlockSpec((NSA_WIDTH, D_MODEL), lambda i: (0, 0)),
        ],
        out_specs=pl.BlockSpec((ROW_TILE, D_MODEL), lambda i: (i, 0)),
        compiler_params=_cparams(("arbitrary",)),
        name="out_proj",
    )(x2, ya, yb, wa, wb)


def _rope_tables(pos):
    half = ROPE_DIM // 2
    inv = ROPE_THETA ** (-jnp.arange(half, dtype=jnp.float32) / half)
    ang = pos.astype(jnp.float32)[:, None] * inv[None, :]
    cos, sin = jnp.cos(ang), jnp.sin(ang)
    n = pos.shape[0]
    rest = HEAD_DIM - ROPE_DIM
    ra = jnp.concatenate([cos, cos, jnp.ones((n, rest), jnp.float32)], axis=1)
    rm = jnp.concatenate([-sin, jnp.zeros((n, half + rest), jnp.float32)], axis=1)
    rp = jnp.concatenate([jnp.zeros((n, half), jnp.float32), sin, jnp.zeros((n, rest), jnp.float32)], axis=1)
    return tuple(jnp.tile(t, (1, LANE // HEAD_DIM)) for t in (ra, rm, rp))


def _block_diag(blocks):
    n, r, c = blocks.shape
    eye = jnp.eye(n, dtype=blocks.dtype)
    return (eye[:, None, :, None] * blocks[:, :, None, :]).reshape(n * r, n * c)


def _compress_params(pe, w1, w2):
    bf = jnp.bfloat16
    w1t = w1.reshape(2, N_CHUNK_TOK, HEAD_DIM, CMP_HIDDEN)
    dup = lambda w: jax.vmap(lambda m: _block_diag(jnp.stack([m] * N_KV)))(w)
    wa, wb = dup(w1t[0]).astype(bf), dup(w1t[1]).astype(bf)
    pet = jnp.tile(pe.reshape(2, N_CHUNK_TOK, 1, HEAD_DIM), (1, 1, 1, N_KV))
    w2d = _block_diag(jnp.stack([w2] * N_KV)).astype(bf)
    return wa, wb, pet[0], pet[1], w2d


def _layer(x2, B, S, norm_g, w_in, conv_w, conv_b, rg_wr, rg_br, rg_wi, rg_bi, rg_lam,
           q_g, k_g, pe_k, w1_k, w2_k, pe_v, w1_v, w2_v, w_out, tabs_tok, tabs_cmp, ind):
    bf = jnp.bfloat16
    w_in_p = jnp.pad(w_in, ((0, 0), (0, N_PAD - N_IN))).astype(bf)
    tile2 = lambda v: jnp.tile(v, LANE // HEAD_DIM)[None, :]
    y_a, qt, ks, vst, kw, vwt, gt, kcvc, ng = _front(
        x2, B, S, norm_g[None, :], w_in_p,
        conv_w, conv_b[None, :], _block_diag(rg_wr).astype(bf), rg_br[None, :],
        _block_diag(rg_wi).astype(bf), rg_bi[None, :], rg_lam[None, :],
        *tabs_tok, tile2(q_g), tile2(k_g[1]), tile2(k_g[2]), ind)
    kc, vct = _compress(kcvc, B, S, _compress_params(pe_k, w1_k, w2_k), _compress_params(pe_v, w1_v, w2_v),
                        tile2(k_g[0]), ind, *tabs_cmp)
    y_b = _nsa(ng, qt, kc, vct, ks, vst, kw, vwt, gt, B, S)

    w_out16 = w_out.astype(bf)
    return _out_proj(x2, y_a, y_b, w_out16[:RG_WIDTH], w_out16[RG_WIDTH:])


def kernel(x, norm_g, w_in, conv_w, conv_b, rg_wr, rg_br, rg_wi, rg_bi, rg_lambda, q_norm_g, k_norm_g,
           cmp_pe_k, cmp_w1_k, cmp_w2_k, cmp_pe_v, cmp_w1_v, cmp_w2_v, w_out):
    B, S, D = x.shape
    assert D == D_MODEL and S % ROW_TILE == 0 and S % SEL_TILE == 0 and S >= WINDOW
    depth = norm_g.shape[0]
    tabs_tok = _rope_tables(jnp.arange(S))
    tabs_cmp = _rope_tables(jnp.arange(S // CMP_STRIDE) * CMP_STRIDE + (CMP_LEN - 1))
    ind = _block_diag(jnp.ones((LANE // HEAD_DIM, HEAD_DIM, HEAD_DIM), jnp.bfloat16))
    x2 = x.reshape(B * S, D)
    for l in range(depth):
        x2 = _layer(x2, B, S, norm_g[l], w_in[l], conv_w[l], conv_b[l], rg_wr[l], rg_br[l], rg_wi[l],
                    rg_bi[l], rg_lambda[l], q_norm_g[l], k_norm_g[l], cmp_pe_k[l], cmp_w1_k[l],
                    cmp_w2_k[l], cmp_pe_v[l], cmp_w1_v[l], cmp_w2_v[l], w_out[l], tabs_tok, tabs_cmp, ind)
    return x2.reshape(B, S, D)
```

```python
import functools
import math

import numpy as np
import jax
import jax.numpy as jnp
from jax import lax
from jax.experimental import pallas as pl
from jax.experimental.pallas import tpu as pltpu

D_MODEL = 1024
RG_WIDTH = 512
RG_BLOCKS = 8
RG_BLOCK = 64
CONV_W = 4
RG_C = 8.0
NSA_WIDTH = 512
HEAD_DIM = 64
N_KV = 2
HPG = 4
KV_WIDTH = N_KV * HEAD_DIM
ROPE_DIM = 16
ROPE_THETA = 500000.0
CMP_LEN = 32
CMP_STRIDE = 16
CMP_HIDDEN = 128
SEL_LEN = 64
SEL_TOPN = 16
WINDOW = 512
N_BRANCH = 3
Q_BLOCK = 128
EPS = 1e-6
NEG = -1e30
N_IN = 2 * RG_WIDTH + 2 * NSA_WIDTH + 6 * KV_WIDTH + N_BRANCH * 8
LANE = 128
N_PAD = ((N_IN + LANE - 1) // LANE) * LANE

COL_Q = 2 * RG_WIDTH
COL_KC = COL_Q + NSA_WIDTH
COL_VC = COL_KC + KV_WIDTH
COL_KS = COL_VC + KV_WIDTH
COL_VS = COL_KS + KV_WIDTH
COL_KW = COL_VS + KV_WIDTH
COL_VW = COL_KW + KV_WIDTH
COL_NG = COL_VW + KV_WIDTH
COL_BG = COL_NG + NSA_WIDTH

VMEM_LIMIT = 56 * 1024 * 1024

ROW_TILE = 512
SEL_TILE = 512
SEL_BLOCKS = SEL_TILE // SEL_LEN
WIN_TILE = 128
LOG2E = 1.4426950408889634


def _cparams(sem):
    return pltpu.CompilerParams(dimension_semantics=sem, vmem_limit_bytes=VMEM_LIMIT)


def _head_mean_sq(x, ind):
    sq = x * x
    hi = sq.astype(jnp.bfloat16)
    lo = (sq - hi.astype(jnp.float32)).astype(jnp.bfloat16)
    tot = (jnp.dot(hi, ind, preferred_element_type=jnp.float32)
           + jnp.dot(lo, ind, preferred_element_type=jnp.float32))
    return tot * (1.0 / HEAD_DIM)


def _norm_rope(x, gain, ind, ra, rm, rp):
    width = x.shape[-1]
    y = (x * lax.rsqrt(_head_mean_sq(x, ind) + EPS)) * gain
    up = pltpu.roll(y, width - ROPE_DIM // 2, axis=1)
    dn = pltpu.roll(y, ROPE_DIM // 2, axis=1)
    return y * ra + up * rm + dn * rp


def _prep_q(zq, first, tabs, qg_ref, ind_ref, qt_ref):
    scale = LOG2E / math.sqrt(HEAD_DIM)
    for c in range(zq.shape[1] // LANE):
        qn = _norm_rope(zq[:, c * LANE:(c + 1) * LANE], qg_ref[...], ind_ref[...], *tabs) * scale
        qt_ref[0, (first + c) * LANE:(first + c + 1) * LANE, :] = qn.T.astype(jnp.bfloat16)


def _prep_kv(zkv, tabs, kg_ref, ind_ref, k_ref, vt_ref):
    k_ref[0] = _norm_rope(zkv[:, :KV_WIDTH], kg_ref[...], ind_ref[...], *tabs).astype(jnp.bfloat16)
    vt_ref[0] = zkv[:, KV_WIDTH:].T.astype(jnp.bfloat16)


RG_TILE = 512
RG_CHUNKS = 8
SUB = 8


def _rglru_tile(x, gate, cw_ref, cb_ref, wr_ref, br_ref, wi_ref, bi_ref, lam_ref,
                y_ref, tail_ref, carry_ref, a_ref, b_ref):
    tail = tail_ref[...]
    row8 = lax.broadcasted_iota(jnp.int32, (SUB, RG_WIDTH), 0)
    conv = x * cw_ref[CONV_W - 1:CONV_W, :] + cb_ref[...]
    for s in range(1, CONV_W):
        xs = pltpu.roll(x, s, axis=0)
        first = jnp.where(row8 < s, pltpu.roll(tail, s, axis=0), xs[0:SUB])
        xs = jnp.concatenate([first, xs[SUB:]], axis=0)
        conv = conv + xs * cw_ref[CONV_W - 1 - s:CONV_W - s, :]
    tail_ref[...] = x[RG_TILE - SUB:]
    yield

    softplus = jax.nn.softplus(-lam_ref[...])
    chunk = RG_TILE // RG_CHUNKS
    grp = (chunk // SUB, SUB, RG_WIDTH)
    row = lax.broadcasted_iota(jnp.int32, grp, 1)
    for c in range(RG_CHUNKS):
        rows = slice(c * chunk, (c + 1) * chunk)
        cv = conv[rows]
        cb16 = cv.astype(jnp.bfloat16)
        r = jax.nn.sigmoid(jnp.dot(cb16, wr_ref[...], preferred_element_type=jnp.float32) + br_ref[...])
        ig = jax.nn.sigmoid(jnp.dot(cb16, wi_ref[...], preferred_element_type=jnp.float32) + bi_ref[...])
        log_a = (-RG_C * r) * softplus
        a = jnp.exp(log_a)
        u = jnp.sqrt(-jnp.tanh(log_a) * (a * a + 1.0)) * (ig * cv)
        a, u = a.reshape(grp), u.reshape(grp)
        for d in (1, 2, 4):
            keep = row >= d
            a_s = jnp.where(keep, pltpu.roll(a, d, axis=1), 1.0)
            u_s = jnp.where(keep, pltpu.roll(u, d, axis=1), 0.0)
            u = a * u_s + u
            a = a * a_s
        a_ref[rows, :] = a.reshape(chunk, RG_WIDTH)
        b_ref[rows, :] = u.reshape(chunk, RG_WIDTH)
        yield

    h = carry_ref[...]
    for c in range(RG_CHUNKS):
        for k in range(c * chunk // SUB, (c + 1) * chunk // SUB):
            grows = slice(k * SUB, (k + 1) * SUB)
            hk = a_ref[grows, :] * h + b_ref[grows, :]
            b_ref[grows, :] = hk
            h = jnp.broadcast_to(hk[SUB - 1:SUB, :], (SUB, RG_WIDTH))
        yield
    carry_ref[...] = h

    for c in range(RG_CHUNKS):
        rows = slice(c * chunk, (c + 1) * chunk)
        gt = gate[rows]
        y_ref[rows, :] = (b_ref[rows, :] * (gt * jax.nn.sigmoid(gt))).astype(y_ref.dtype)
        yield


def _front_body(x_ref, g_ref, w_ref,
                cw_ref, cb_ref, wr_ref, br_ref, wi_ref, bi_ref, lam_ref,
                ra_ref, rm_ref, rp_ref, qg_ref, kgs_ref, kgw_ref, ind_ref,
                ya_ref, qt_ref, kso_ref, vst_ref, kwo_ref, vwt_ref, gt_ref, kcvc_ref, ng_ref,
                tail_ref, carry_ref, a_ref, b_ref):
    @pl.when(pl.program_id(1) == 0)
    def _():
        tail_ref[...] = jnp.zeros_like(tail_ref)
        carry_ref[...] = jnp.zeros_like(carry_ref)

    x = x_ref[...]
    ms = jnp.mean(x * x, axis=-1, keepdims=True)
    h = ((x * lax.rsqrt(ms + EPS)) * g_ref[...]).astype(jnp.bfloat16)
    proj = lambda c0, c1: jnp.dot(h, w_ref[:, c0:c1], preferred_element_type=jnp.float32)

    z_rg = proj(0, COL_Q)
    mixer = _rglru_tile(z_rg[:, :RG_WIDTH], z_rg[:, RG_WIDTH:], cw_ref, cb_ref, wr_ref, br_ref, wi_ref,
                        bi_ref, lam_ref, ya_ref, tail_ref, carry_ref, a_ref, b_ref)
    tabs = (ra_ref[...], rm_ref[...], rp_ref[...])
    half = 2 * KV_WIDTH
    z = {}

    def project(name, c0, c1):
        z[name] = proj(c0, c1)

    def store(ref, c0, c1):
        ref[...] = proj(c0, c1)

    def gates():
        gt_ref[0] = jax.nn.sigmoid(proj(COL_BG, N_PAD)).T

    items = [
        functools.partial(project, "q0", COL_Q, COL_Q + half),
        functools.partial(project, "q1", COL_Q + half, COL_KC),
        functools.partial(project, "sel", COL_KS, COL_KW),
        functools.partial(project, "win", COL_KW, COL_NG),
        functools.partial(store, kcvc_ref, COL_KC, COL_KS),
        lambda: _prep_q(z["q0"], 0, tabs, qg_ref, ind_ref, qt_ref),
        functools.partial(store, ng_ref.at[:, 0:half], COL_NG, COL_NG + half),
        lambda: _prep_q(z["q1"], half // LANE, tabs, qg_ref, ind_ref, qt_ref),
        functools.partial(store, ng_ref.at[:, half:], COL_NG + half, COL_BG),
        lambda: _prep_kv(z["sel"], tabs, kgs_ref, ind_ref, kso_ref, vst_ref),
        gates,
        lambda: _prep_kv(z["win"], tabs, kgw_ref, ind_ref, kwo_ref, vwt_ref),
    ]
    every = (1 + 3 * RG_CHUNKS) // len(items)
    for i, _ in enumerate(mixer):
        if items and i % every == 0:
            items.pop(0)()
    while items:
        items.pop(0)()


def _front(x2, B, S, g, w, cw, cb, wr, br, wi, bi, lam, ra, rm, rp, qg, kgs, kgw, ind):
    nt = S // ROW_TILE
    bf = jnp.bfloat16
    rows = lambda width: pl.BlockSpec((ROW_TILE, width), lambda b, i: (b * nt + i, 0))
    const = lambda a: pl.BlockSpec(a.shape, lambda b, i: (0, 0))
    tab = pl.BlockSpec((ROW_TILE, LANE), lambda b, i: (i, 0))
    tok = lambda width: pl.BlockSpec((1, ROW_TILE, width), lambda b, i: (b, i, 0))
    tr = lambda height: pl.BlockSpec((1, height, ROW_TILE), lambda b, i: (b, 0, i))
    return pl.pallas_call(
        _front_body,
        out_shape=(
            jax.ShapeDtypeStruct((B * S, RG_WIDTH), bf),
            jax.ShapeDtypeStruct((B, NSA_WIDTH, S), bf),
            jax.ShapeDtypeStruct((B, S, KV_WIDTH), bf),
            jax.ShapeDtypeStruct((B, KV_WIDTH, S), bf),
            jax.ShapeDtypeStruct((B, S, KV_WIDTH), bf),
            jax.ShapeDtypeStruct((B, KV_WIDTH, S), bf),
            jax.ShapeDtypeStruct((B, LANE, S), jnp.float32),
            jax.ShapeDtypeStruct((B * S, 2 * KV_WIDTH), jnp.float32),
            jax.ShapeDtypeStruct((B * S, NSA_WIDTH), jnp.float32),
        ),
        grid=(B, nt),
        in_specs=[rows(D_MODEL), const(g), const(w),
                  const(cw), const(cb), const(wr), const(br), const(wi), const(bi), const(lam),
                  tab, tab, tab, const(qg), const(kgs), const(kgw), const(ind)],
        out_specs=(rows(RG_WIDTH), tr(NSA_WIDTH), tok(KV_WIDTH), tr(KV_WIDTH), tok(KV_WIDTH), tr(KV_WIDTH),
                   tr(LANE), rows(2 * KV_WIDTH), rows(NSA_WIDTH)),
        scratch_shapes=[
            pltpu.VMEM((SUB, RG_WIDTH), jnp.float32),
            pltpu.VMEM((SUB, RG_WIDTH), jnp.float32),
            pltpu.VMEM((RG_TILE, RG_WIDTH), jnp.float32),
            pltpu.VMEM((RG_TILE, RG_WIDTH), jnp.float32),
        ],
        compiler_params=_cparams(("arbitrary", "arbitrary")),
        name="front",
    )(x2, g, w, cw, cb, wr, br, wi, bi, lam, ra, rm, rp, qg, kgs, kgw, ind)


N_CHUNK_TOK = CMP_STRIDE


def _compress_one(src_ref, wa_ref, wb_ref, pea_ref, peb_ref, w2_ref, n_chunk):
    ha = jnp.zeros((n_chunk, 2 * CMP_HIDDEN), jnp.float32)
    hb = jnp.zeros((n_chunk, 2 * CMP_HIDDEN), jnp.float32)
    for j in range(N_CHUNK_TOK):
        xj = src_ref[pl.ds(j, n_chunk, stride=N_CHUNK_TOK), :]
        ha = ha + jnp.dot((xj + pea_ref[j]).astype(jnp.bfloat16), wa_ref[j],
                          preferred_element_type=jnp.float32)
        hb = hb + jnp.dot((xj + peb_ref[j]).astype(jnp.bfloat16), wb_ref[j],
                          preferred_element_type=jnp.float32)
    hid = ha + pltpu.roll(hb, n_chunk - 1, axis=0)
    act = hid * jax.nn.sigmoid(hid)
    return jnp.dot(act.astype(jnp.bfloat16), w2_ref[...], preferred_element_type=jnp.float32)


def _compress_body(kc_ref, vc_ref, wak_ref, wbk_ref, peak_ref, pebk_ref, w2k_ref,
                   wav_ref, wbv_ref, peav_ref, pebv_ref, w2v_ref,
                   kg_ref, ind_ref, ra_ref, rm_ref, rp_ref, ko_ref, vto_ref, *, n_chunk):
    kc = _compress_one(kc_ref, wak_ref, wbk_ref, peak_ref, pebk_ref, w2k_ref, n_chunk)
    ko_ref[0] = _norm_rope(kc, kg_ref[...], ind_ref[...], ra_ref[...], rm_ref[...],
                           rp_ref[...]).astype(jnp.bfloat16)
    vc = _compress_one(vc_ref, wav_ref, wbv_ref, peav_ref, pebv_ref, w2v_ref, n_chunk)
    vto_ref[0] = vc.T.astype(jnp.bfloat16)


def _compress(z, B, S, wk, wv, kg, ind, ra, rm, rp):
    n_chunk = S // N_CHUNK_TOK

    def full(a):
        nd = a.ndim
        return pl.BlockSpec(a.shape, lambda b: (0,) * nd)

    consts = list(wk) + list(wv) + [kg, ind, ra, rm, rp]
    return pl.pallas_call(
        functools.partial(_compress_body, n_chunk=n_chunk),
        out_shape=(
            jax.ShapeDtypeStruct((B, n_chunk, KV_WIDTH), jnp.bfloat16),
            jax.ShapeDtypeStruct((B, KV_WIDTH, n_chunk), jnp.bfloat16),
        ),
        grid=(B,),
        in_specs=[
            pl.BlockSpec((S, KV_WIDTH), lambda b: (b, 0)),
            pl.BlockSpec((S, KV_WIDTH), lambda b: (b, 1)),
        ] + [full(a) for a in consts],
        out_specs=(
            pl.BlockSpec((1, n_chunk, KV_WIDTH), lambda b: (b, 0, 0)),
            pl.BlockSpec((1, KV_WIDTH, n_chunk), lambda b: (b, 0, 0)),
        ),
        compiler_params=_cparams(("arbitrary",)),
        name="compress",
    )(z, z, *consts)


G_COL = HPG * Q_BLOCK
N_COL = N_KV * G_COL


def _group_dots(vts, p):
    p16 = p.astype(jnp.bfloat16)
    return jnp.concatenate(
        [jnp.dot(vt, p16[:, g * G_COL:(g + 1) * G_COL], preferred_element_type=jnp.float32)
         for g, vt in enumerate(vts)], axis=1)


MXU_COLS = 256
ONES_ROWS = 16
ACC_ROWS = HEAD_DIM + ONES_ROWS


def _with_ones(vt):
    return jnp.concatenate([vt, jnp.ones((ONES_ROWS, vt.shape[1]), vt.dtype)], axis=0)


def _tile_softmax(s, vts):
    vones = [_with_ones(vt) for vt in vts]
    ms, accs = [], []
    for c in range(N_COL // MXU_COLS):
        sc = s[:, c * MXU_COLS:(c + 1) * MXU_COLS]
        m = jnp.max(sc, axis=0, keepdims=True)
        p = jnp.exp2(sc - m).astype(jnp.bfloat16)
        ms.append(m)
        accs.append(jnp.dot(vones[c * MXU_COLS // G_COL], p, preferred_element_type=jnp.float32))
    return jnp.concatenate(ms, axis=1), jnp.concatenate(accs, axis=1)


def _merge(state, tile):
    m0, acc0 = state
    m1, acc1 = tile
    m = jnp.maximum(m0, m1)
    return m, jnp.exp2(m0 - m) * acc0 + jnp.exp2(m1 - m) * acc1


def _finish(state):
    _, acc = state
    return acc[0:HEAD_DIM] * (1.0 / acc[HEAD_DIM:HEAD_DIM + 1])


def _nsa_body(qt_ref, kc_ref, vct_ref, ks_ref, vst_ref, kw_ref, vwt_ref, gt_ref, ng0_ref, ng1_ref, e_ref,
              y_ref, ps_ref, bias_ref, sa_ref, sb_ref, *, n_cmp_pad):
    qi = pl.program_id(1)
    q0 = qi * Q_BLOCK
    groups = range(N_KV)
    grows = lambda g: slice(g * HEAD_DIM, (g + 1) * HEAD_DIM)
    gcols = lambda g: slice(g * G_COL, (g + 1) * G_COL)

    qt = qt_ref[0]
    zero = jnp.zeros((HEAD_DIM, G_COL), jnp.bfloat16)
    qrows = []
    for g in groups:
        qcat = jnp.concatenate([qt[(g * HPG + h) * HEAD_DIM:(g * HPG + h + 1) * HEAD_DIM, :]
                                for h in range(HPG)], axis=1)
        qrows.append(jnp.concatenate([qcat if gg == g else zero for gg in groups], axis=1))
    qz = jnp.concatenate(qrows, axis=0)

    tl = lax.broadcasted_iota(jnp.int32, (1, N_COL), 1) % Q_BLOCK
    tcol = q0 + tl

    per = SEL_LEN // CMP_STRIDE
    n_sel_all = n_cmp_pad // per

    def compressed_and_topk(n_cmp):
        sc = jnp.dot(kc_ref[0, 0:n_cmp, :], qz, preferred_element_type=jnp.float32)
        crow = lax.broadcasted_iota(jnp.int32, (n_cmp, N_COL), 0)
        cvalid = crow <= (tcol - (CMP_LEN - 1)) // CMP_STRIDE
        scm = jnp.where(cvalid, sc, NEG)
        mc = jnp.max(scm, axis=0, keepdims=True)
        pc = jnp.exp2(scm - mc)
        lc = jnp.sum(pc, axis=0, keepdims=True)
        pc = pc * jnp.where(tcol >= CMP_LEN - 1, 1.0 / lc, 0.0)
        o_c = _group_dots([vct_ref[0, grows(g), 0:n_cmp] for g in groups], pc)

        n_sel = n_cmp // per
        tq = q0 + lax.broadcasted_iota(jnp.int32, (n_sel, Q_BLOCK), 1)
        bt = tq // SEL_LEN
        rowj = lax.broadcasted_iota(jnp.int32, (n_sel, Q_BLOCK), 0)
        forced = (rowj == 0) | (rowj == bt) | (rowj == bt - 1)
        valid = rowj <= bt
        rowf = rowj.astype(jnp.float32)
        cands = []
        for g in groups:
            psum = pc[:, g * G_COL:g * G_COL + Q_BLOCK]
            for h in range(1, HPG):
                psum = psum + pc[:, g * G_COL + h * Q_BLOCK:g * G_COL + (h + 1) * Q_BLOCK]
            ps_ref[g, 0:SUB, :] = jnp.zeros((SUB, Q_BLOCK), jnp.float32)
            ps_ref[g, SUB:SUB + n_cmp, :] = psum
            strided = lambda k: ps_ref[g, pl.ds(SUB + k, n_sel, stride=per), :]
            imp = strided(0) + strided(1) + strided(2) + 0.5 * strided(3) + 0.5 * strided(-1)
            cands.append(jnp.where(valid & jnp.logical_not(forced), imp, -1.0))

        eligible = [c >= 0.0 for c in cands]
        for _ in range(SEL_TOPN - 3):
            for g in groups:
                mx = jnp.max(cands[g], axis=0, keepdims=True)
                first = jnp.min(jnp.where(cands[g] == mx, rowf, float(n_sel)), axis=0, keepdims=True)
                cands[g] = jnp.where(rowf == first, -1.0, cands[g])
        taken = [(forced & valid) | (eligible[g] & (cands[g] < 0.0)) for g in groups]
        bias_ref[0:n_sel, :] = jnp.concatenate([jnp.where(tk, 0.0, NEG) for tk in taken], axis=1)
        if n_sel < n_sel_all:
            bias_ref[n_sel:n_sel_all, :] = jnp.full((n_sel_all - n_sel, N_KV * Q_BLOCK), NEG, jnp.float32)
        bias_ref[n_sel_all:, :] = jnp.zeros((LANE, N_KV * Q_BLOCK), jnp.float32)
        return o_c

    n_var = 4
    quarter = n_cmp_pad // n_var
    variant = jnp.minimum((q0 + Q_BLOCK - CMP_LEN) // CMP_STRIDE // quarter, n_var - 1)
    o_c = lax.switch(variant, [functools.partial(compressed_and_topk, (v + 1) * quarter) for v in range(n_var)])

    e_mat = e_ref[...]

    def sel_scores(t):
        koff = pl.multiple_of(t * SEL_TILE, SEL_TILE)
        boff = pl.multiple_of(t * SEL_BLOCKS, SEL_BLOCKS)
        brows = bias_ref[pl.ds(boff, LANE), :].astype(jnp.bfloat16)
        bcols = jnp.concatenate([brows[:, g * Q_BLOCK:(g + 1) * Q_BLOCK] for g in groups for _ in range(HPG)],
                                axis=1)
        rhs = jnp.concatenate([qz, bcols], axis=0)
        lhs = jnp.concatenate([ks_ref[0, pl.ds(koff, SEL_TILE), :], e_mat], axis=1)
        return jnp.dot(lhs, rhs, preferred_element_type=jnp.float32)

    def sel_values(t):
        koff = pl.multiple_of(t * SEL_TILE, SEL_TILE)
        return [vst_ref[0, grows(g), pl.ds(koff, SEL_TILE)] for g in groups]

    def pair(j, state):
        t = 2 * j
        sb_ref[...] = sel_scores(t + 1)
        state = _merge(state, _tile_softmax(sa_ref[...], sel_values(t)))
        sa_ref[...] = sel_scores(t + 2)
        return _merge(state, _tile_softmax(sb_ref[...], sel_values(t + 1)))

    def last_tile(s_ref, t, state):
        srow = lax.broadcasted_iota(jnp.int32, (SEL_TILE, N_COL), 0)
        s = jnp.where(srow <= tcol - t * SEL_TILE, s_ref[...], NEG)
        return _finish(_merge(state, _tile_softmax(s, sel_values(t))))

    n_open = q0 // SEL_TILE
    sa_ref[...] = sel_scores(0)

    win_len = WINDOW + Q_BLOCK
    woff = pl.multiple_of(q0, Q_BLOCK)
    flag = jnp.where(lax.broadcasted_iota(jnp.int32, (LANE, N_COL), 0) == 0, NEG, 0.0)
    sw = jnp.dot(kw_ref[0, pl.ds(woff, win_len), :],
                 jnp.concatenate([qz, flag.astype(jnp.bfloat16)], axis=0),
                 preferred_element_type=jnp.float32)
    wrow = lax.broadcasted_iota(jnp.int32, (Q_BLOCK, N_COL), 0)
    sw = jnp.concatenate([jnp.where(wrow > tl, sw[0:Q_BLOCK], NEG),
                          sw[Q_BLOCK:WINDOW],
                          jnp.where(wrow <= tl, sw[WINDOW:], NEG)], axis=0)
    o_w = _finish(_tile_softmax(sw, [vwt_ref[0, grows(g), pl.ds(woff, win_len)] for g in groups]))

    state = (jnp.full((1, N_COL), NEG, jnp.float32), jnp.zeros((ACC_ROWS, N_COL), jnp.float32))
    n_pair = n_open // 2
    state = lax.fori_loop(0, n_pair // 2, lambda j, st: pair(2 * j + 1, pair(2 * j, st)), state)
    state = lax.fori_loop(2 * (n_pair // 2), n_pair, pair, state)
    t_next = 2 * (n_open // 2)

    def odd_tail(state):
        sb_ref[...] = sel_scores(t_next + 1)
        state = _merge(state, _tile_softmax(sa_ref[...], sel_values(t_next)))
        return last_tile(sb_ref, t_next + 1, state)

    def even_tail(state):
        return last_tile(sa_ref, t_next, state)

    o_s = lax.cond(n_open % 2 == 1, odd_tail, even_tail, state)

    def gate_row(branch):
        return jnp.concatenate(
            [gt_ref[0, r * N_BRANCH + branch:r * N_BRANCH + branch + 1, :] for r in range(N_KV * HPG)], axis=1)

    yt = gate_row(0) * o_c + gate_row(1) * o_s + gate_row(2) * o_w
    halves = []
    for pr in range(N_KV * HPG // 2):
        two = jnp.concatenate([yt[:, (2 * pr) * Q_BLOCK:(2 * pr + 1) * Q_BLOCK],
                               yt[:, (2 * pr + 1) * Q_BLOCK:(2 * pr + 2) * Q_BLOCK]], axis=0)
        halves.append(two.T)
    y = jnp.concatenate(halves, axis=1)
    ng = jnp.concatenate([ng0_ref[...], ng1_ref[...]], axis=1)
    y_ref[...] = (y * (ng * jax.nn.sigmoid(ng))).astype(y_ref.dtype)


def _nsa(z, qt, kc, vct, ks, vst, kw, vwt, gt, B, S):
    nq = S // Q_BLOCK
    n_cmp_pad = S // CMP_STRIDE
    gw = HPG * HEAD_DIM
    bf = jnp.bfloat16
    e_mat = (jnp.arange(SEL_TILE)[:, None] // SEL_LEN == jnp.arange(LANE)[None, :]).astype(bf)
    lead = jnp.zeros((B, WINDOW, 2 * KV_WIDTH), bf).at[:, :, KV_WIDTH].set(1.0)
    kw_pad = jnp.concatenate([lead, jnp.concatenate([kw, jnp.zeros_like(kw)], axis=2)], axis=1)
    vwt_pad = jnp.pad(vwt, ((0, 0), (0, 0), (WINDOW, 0)))
    whole = lambda a: pl.BlockSpec((1,) + a.shape[1:], lambda b, i: (b, 0, 0))
    return pl.pallas_call(
        functools.partial(_nsa_body, n_cmp_pad=n_cmp_pad),
        out_shape=jax.ShapeDtypeStruct((B * S, NSA_WIDTH), bf),
        grid=(B, nq),
        in_specs=[
            pl.BlockSpec((1, NSA_WIDTH, Q_BLOCK), lambda b, i: (b, 0, i)),
            whole(kc), whole(vct), whole(ks), whole(vst), whole(kw_pad), whole(vwt_pad),
            pl.BlockSpec((1, LANE, Q_BLOCK), lambda b, i: (b, 0, i)),
            pl.BlockSpec((Q_BLOCK, gw), lambda b, i: (b * nq + i, 0)),
            pl.BlockSpec((Q_BLOCK, gw), lambda b, i: (b * nq + i, 1)),
            pl.BlockSpec((SEL_TILE, LANE), lambda b, i: (0, 0)),
        ],
        out_specs=pl.BlockSpec((Q_BLOCK, NSA_WIDTH), lambda b, i: (b * nq + i, 0)),
        scratch_shapes=[
            pltpu.VMEM((N_KV, SUB + n_cmp_pad, Q_BLOCK), jnp.float32),
            pltpu.VMEM((S // SEL_LEN + LANE, N_KV * Q_BLOCK), jnp.float32),
            pltpu.VMEM((SEL_TILE, N_COL), jnp.float32),
            pltpu.VMEM((SEL_TILE, N_COL), jnp.float32),
        ],
        compiler_params=_cparams(("arbitrary", "arbitrary")),
        name="nsa",
    )(qt, kc, vct, ks, vst, kw_pad, vwt_pad, gt, z, z, e_mat)


def _out_proj_body(x_ref, ya_ref, yb_ref, wa_ref, wb_ref, o_ref):
    o_ref[...] = (x_ref[...]
                  + jnp.dot(ya_ref[...], wa_ref[...], preferred_element_type=jnp.float32)
                  + jnp.dot(yb_ref[...], wb_ref[...], preferred_element_type=jnp.float32))


def _out_proj(x2, ya, yb, wa, wb):
    rows = x2.shape[0]
    return pl.pallas_call(
        _out_proj_body,
        out_shape=jax.ShapeDtypeStruct((rows, D_MODEL), jnp.float32),
        grid=(rows // ROW_TILE,),
        in_specs=[
            pl.BlockSpec((ROW_TILE, D_MODEL), lambda i: (i, 0)),
            pl.BlockSpec((ROW_TILE, RG_WIDTH), lambda i: (i, 0)),
            pl.BlockSpec((ROW_TILE, NSA_WIDTH), lambda i: (i, 0)),
            pl.BlockSpec((RG_WIDTH, D_MODEL), lambda i: (0, 0)),
            pl.BlockSpec((NSA_WIDTH, D_MODEL), lambda i: (0, 0)),
        ],
        out_specs=pl.BlockSpec((ROW_TILE, D_MODEL), lambda i: (i, 0)),
        compiler_params=_cparams(("arbitrary",)),
        name="out_proj",
    )(x2, ya, yb, wa, wb)


def _rope_tables(pos):
    half = ROPE_DIM // 2
    inv = ROPE_THETA ** (-jnp.arange(half, dtype=jnp.float32) / half)
    ang = pos.astype(jnp.float32)[:, None] * inv[None, :]
    cos, sin = jnp.cos(ang), jnp.sin(ang)
    n = pos.shape[0]
    rest = HEAD_DIM - ROPE_DIM
    ra = jnp.concatenate([cos, cos, jnp.ones((n, rest), jnp.float32)], axis=1)
    rm = jnp.concatenate([-sin, jnp.zeros((n, half + rest), jnp.float32)], axis=1)
    rp = jnp.concatenate([jnp.zeros((n, half), jnp.float32), sin, jnp.zeros((n, rest), jnp.float32)], axis=1)
    return tuple(jnp.tile(t, (1, LANE // HEAD_DIM)) for t in (ra, rm, rp))


def _block_diag(blocks):
    n, r, c = blocks.shape
    eye = jnp.eye(n, dtype=blocks.dtype)
    return (eye[:, None, :, None] * blocks[:, :, None, :]).reshape(n * r, n * c)


def _compress_params(pe, w1, w2):
    bf = jnp.bfloat16
    w1t = w1.reshape(2, N_CHUNK_TOK, HEAD_DIM, CMP_HIDDEN)
    dup = lambda w: jax.vmap(lambda m: _block_diag(jnp.stack([m] * N_KV)))(w)
    wa, wb = dup(w1t[0]).astype(bf), dup(w1t[1]).astype(bf)
    pet = jnp.tile(pe.reshape(2, N_CHUNK_TOK, 1, HEAD_DIM), (1, 1, 1, N_KV))
    w2d = _block_diag(jnp.stack([w2] * N_KV)).astype(bf)
    return wa, wb, pet[0], pet[1], w2d


def _layer(x2, B, S, norm_g, w_in, conv_w, conv_b, rg_wr, rg_br, rg_wi, rg_bi, rg_lam,
           q_g, k_g, pe_k, w1_k, w2_k, pe_v, w1_v, w2_v, w_out, tabs_tok, tabs_cmp, ind):
    bf = jnp.bfloat16
    w_in_p = jnp.pad(w_in, ((0, 0), (0, N_PAD - N_IN))).astype(bf)
    tile2 = lambda v: jnp.tile(v, LANE // HEAD_DIM)[None, :]
    y_a, qt, ks, vst, kw, vwt, gt, kcvc, ng = _front(
        x2, B, S, norm_g[None, :], w_in_p,
        conv_w, conv_b[None, :], _block_diag(rg_wr).astype(bf), rg_br[None, :],
        _block_diag(rg_wi).astype(bf), rg_bi[None, :], rg_lam[None, :],
        *tabs_tok, tile2(q_g), tile2(k_g[1]), tile2(k_g[2]), ind)
    kc, vct = _compress(kcvc, B, S, _compress_params(pe_k, w1_k, w2_k), _compress_params(pe_v, w1_v, w2_v),
                        tile2(k_g[0]), ind, *tabs_cmp)
    y_b = _nsa(ng, qt, kc, vct, ks, vst, kw, vwt, gt, B, S)

    w_out16 = w_out.astype(bf)
    return _out_proj(x2, y_a, y_b, w_out16[:RG_WIDTH], w_out16[RG_WIDTH:])


def kernel(x, norm_g, w_in, conv_w, conv_b, rg_wr, rg_br, rg_wi, rg_bi, rg_lambda, q_norm_g, k_norm_g,
           cmp_pe_k, cmp_w1_k, cmp_w2_k, cmp_pe_v, cmp_w1_v, cmp_w2_v, w_out):
    B, S, D = x.shape
    assert D == D_MODEL and S % ROW_TILE == 0 and S % SEL_TILE == 0 and S >= WINDOW
    depth = norm_g.shape[0]
    tabs_tok = _rope_tables(jnp.arange(S))
    tabs_cmp = _rope_tables(jnp.arange(S // CMP_STRIDE) * CMP_STRIDE + (CMP_LEN - 1))
    ind = _block_diag(jnp.ones((LANE // HEAD_DIM, HEAD_DIM, HEAD_DIM), jnp.bfloat16))
    x2 = x.reshape(B * S, D)
    for l in range(depth):
        x2 = _layer(x2, B, S, norm_g[l], w_in[l], conv_w[l], conv_b[l], rg_wr[l], rg_br[l], rg_wi[l],
                    rg_bi[l], rg_lambda[l], q_norm_g[l], k_norm_g[l], cmp_pe_k[l], cmp_w1_k[l],
                    cmp_w2_k[l], cmp_pe_v[l], cmp_w1_v[l], cmp_w2_v[l], w_out[l], tabs_tok, tabs_cmp, ind)
    return x2.reshape(B, S, D)
```

```python
import functools
import math

import numpy as np
import jax
import jax.numpy as jnp
from jax import lax
from jax.experimental import pallas as pl
from jax.experimental.pallas import tpu as pltpu

D_MODEL = 1024
RG_WIDTH = 512
RG_BLOCKS = 8
RG_BLOCK = 64
CONV_W = 4
RG_C = 8.0
NSA_WIDTH = 512
HEAD_DIM = 64
N_KV = 2
HPG = 4
KV_WIDTH = N_KV * HEAD_DIM
ROPE_DIM = 16
ROPE_THETA = 500000.0
CMP_LEN = 32
CMP_STRIDE = 16
CMP_HIDDEN = 128
SEL_LEN = 64
SEL_TOPN = 16
WINDOW = 512
N_BRANCH = 3
Q_BLOCK = 256
EPS = 1e-6
NEG = -1e30
N_IN = 2 * RG_WIDTH + 2 * NSA_WIDTH + 6 * KV_WIDTH + N_BRANCH * 8
LANE = 128
N_PAD = ((N_IN + LANE - 1) // LANE) * LANE

COL_Q = 2 * RG_WIDTH
COL_KC = COL_Q + NSA_WIDTH
COL_VC = COL_KC + KV_WIDTH
COL_KS = COL_VC + KV_WIDTH
COL_VS = COL_KS + KV_WIDTH
COL_KW = COL_VS + KV_WIDTH
COL_VW = COL_KW + KV_WIDTH
COL_NG = COL_VW + KV_WIDTH
COL_BG = COL_NG + NSA_WIDTH

VMEM_LIMIT = 56 * 1024 * 1024

ROW_TILE = 512
SEL_TILE = 512
SEL_BLOCKS = SEL_TILE // SEL_LEN
WIN_TILE = 128
LOG2E = 1.4426950408889634


def _cparams(sem):
    return pltpu.CompilerParams(dimension_semantics=sem, vmem_limit_bytes=VMEM_LIMIT)


def _head_mean_sq(x, ind):
    sq = x * x
    hi = sq.astype(jnp.bfloat16)
    lo = (sq - hi.astype(jnp.float32)).astype(jnp.bfloat16)
    tot = (jnp.dot(hi, ind, preferred_element_type=jnp.float32)
           + jnp.dot(lo, ind, preferred_element_type=jnp.float32))
    return tot * (1.0 / HEAD_DIM)


def _norm_rope(x, gain, ind, ra, rm, rp):
    width = x.shape[-1]
    y = (x * lax.rsqrt(_head_mean_sq(x, ind) + EPS)) * gain
    up = pltpu.roll(y, width - ROPE_DIM // 2, axis=1)
    dn = pltpu.roll(y, ROPE_DIM // 2, axis=1)
    return y * ra + up * rm + dn * rp


def _prep_q(zq, first, tabs, qg_ref, ind_ref, qt_ref):
    scale = LOG2E / math.sqrt(HEAD_DIM)
    for c in range(zq.shape[1] // LANE):
        qn = _norm_rope(zq[:, c * LANE:(c + 1) * LANE], qg_ref[...], ind_ref[...], *tabs) * scale
        qt_ref[0, (first + c) * LANE:(first + c + 1) * LANE, :] = qn.T.astype(jnp.bfloat16)


def _prep_kv(zkv, tabs, kg_ref, ind_ref, k_ref, vt_ref):
    k_ref[0] = _norm_rope(zkv[:, :KV_WIDTH], kg_ref[...], ind_ref[...], *tabs).astype(jnp.bfloat16)
    vt_ref[0] = zkv[:, KV_WIDTH:].T.astype(jnp.bfloat16)


RG_TILE = 512
RG_CHUNKS = 8
SUB = 8


def _rglru_tile(x, gate, cw_ref, cb_ref, wr_ref, br_ref, wi_ref, bi_ref, lam_ref,
                y_ref, tail_ref, carry_ref, a_ref, b_ref):
    tail = tail_ref[...]
    row8 = lax.broadcasted_iota(jnp.int32, (SUB, RG_WIDTH), 0)
    conv = x * cw_ref[CONV_W - 1:CONV_W, :] + cb_ref[...]
    for s in range(1, CONV_W):
        xs = pltpu.roll(x, s, axis=0)
        first = jnp.where(row8 < s, pltpu.roll(tail, s, axis=0), xs[0:SUB])
        xs = jnp.concatenate([first, xs[SUB:]], axis=0)
        conv = conv + xs * cw_ref[CONV_W - 1 - s:CONV_W - s, :]
    tail_ref[...] = x[RG_TILE - SUB:]
    yield

    softplus = jax.nn.softplus(-lam_ref[...])
    chunk = RG_TILE // RG_CHUNKS
    grp = (chunk // SUB, SUB, RG_WIDTH)
    row = lax.broadcasted_iota(jnp.int32, grp, 1)
    for c in range(RG_CHUNKS):
        rows = slice(c * chunk, (c + 1) * chunk)
        cv = conv[rows]
        cb16 = cv.astype(jnp.bfloat16)
        r = jax.nn.sigmoid(jnp.dot(cb16, wr_ref[...], preferred_element_type=jnp.float32) + br_ref[...])
        ig = jax.nn.sigmoid(jnp.dot(cb16, wi_ref[...], preferred_element_type=jnp.float32) + bi_ref[...])
        log_a = (-RG_C * r) * softplus
        a = jnp.exp(log_a)
        u = jnp.sqrt(-jnp.tanh(log_a) * (a * a + 1.0)) * (ig * cv)
        a, u = a.reshape(grp), u.reshape(grp)
        for d in (1, 2, 4):
            keep = row >= d
            a_s = jnp.where(keep, pltpu.roll(a, d, axis=1), 1.0)
            u_s = jnp.where(keep, pltpu.roll(u, d, axis=1), 0.0)
            u = a * u_s + u
            a = a * a_s
        a_ref[rows, :] = a.reshape(chunk, RG_WIDTH)
        b_ref[rows, :] = u.reshape(chunk, RG_WIDTH)
        yield

    h = carry_ref[...]
    for c in range(RG_CHUNKS):
        for k in range(c * chunk // SUB, (c + 1) * chunk // SUB):
            grows = slice(k * SUB, (k + 1) * SUB)
            hk = a_ref[grows, :] * h + b_ref[grows, :]
            b_ref[grows, :] = hk
            h = jnp.broadcast_to(hk[SUB - 1:SUB, :], (SUB, RG_WIDTH))
        yield
    carry_ref[...] = h

    for c in range(RG_CHUNKS):
        rows = slice(c * chunk, (c + 1) * chunk)
        gt = gate[rows]
        y_ref[rows, :] = (b_ref[rows, :] * (gt * jax.nn.sigmoid(gt))).astype(y_ref.dtype)
        yield


def _front_body(x_ref, g_ref, w_ref,
                cw_ref, cb_ref, wr_ref, br_ref, wi_ref, bi_ref, lam_ref,
                ra_ref, rm_ref, rp_ref, qg_ref, kgs_ref, kgw_ref, ind_ref,
                ya_ref, qt_ref, kso_ref, vst_ref, kwo_ref, vwt_ref, gt_ref, kcvc_ref, ng_ref,
                tail_ref, carry_ref, a_ref, b_ref):
    @pl.when(pl.program_id(1) == 0)
    def _():
        tail_ref[...] = jnp.zeros_like(tail_ref)
        carry_ref[...] = jnp.zeros_like(carry_ref)

    x = x_ref[...]
    ms = jnp.mean(x * x, axis=-1, keepdims=True)
    h = ((x * lax.rsqrt(ms + EPS)) * g_ref[...]).astype(jnp.bfloat16)
    proj = lambda c0, c1: jnp.dot(h, w_ref[:, c0:c1], preferred_element_type=jnp.float32)

    z_rg = proj(0, COL_Q)
    mixer = _rglru_tile(z_rg[:, :RG_WIDTH], z_rg[:, RG_WIDTH:], cw_ref, cb_ref, wr_ref, br_ref, wi_ref,
                        bi_ref, lam_ref, ya_ref, tail_ref, carry_ref, a_ref, b_ref)
    tabs = (ra_ref[...], rm_ref[...], rp_ref[...])
    half = 2 * KV_WIDTH
    z = {}

    def project(name, c0, c1):
        z[name] = proj(c0, c1)

    def store(ref, c0, c1):
        ref[...] = proj(c0, c1)

    def gates():
        gt_ref[0] = jax.nn.sigmoid(proj(COL_BG, N_PAD)).T

    items = [
        functools.partial(project, "q0", COL_Q, COL_Q + half),
        functools.partial(project, "q1", COL_Q + half, COL_KC),
        functools.partial(project, "sel", COL_KS, COL_KW),
        functools.partial(project, "win", COL_KW, COL_NG),
        functools.partial(store, kcvc_ref, COL_KC, COL_KS),
        lambda: _prep_q(z["q0"], 0, tabs, qg_ref, ind_ref, qt_ref),
        functools.partial(store, ng_ref.at[:, 0:half], COL_NG, COL_NG + half),
        lambda: _prep_q(z["q1"], half // LANE, tabs, qg_ref, ind_ref, qt_ref),
        functools.partial(store, ng_ref.at[:, half:], COL_NG + half, COL_BG),
        lambda: _prep_kv(z["sel"], tabs, kgs_ref, ind_ref, kso_ref, vst_ref),
        gates,
        lambda: _prep_kv(z["win"], tabs, kgw_ref, ind_ref, kwo_ref, vwt_ref),
    ]
    every = (1 + 3 * RG_CHUNKS) // len(items)
    for i, _ in enumerate(mixer):
        if items and i % every == 0:
            items.pop(0)()
    while items:
        items.pop(0)()


def _front(x2, B, S, g, w, cw, cb, wr, br, wi, bi, lam, ra, rm, rp, qg, kgs, kgw, ind):
    nt = S // ROW_TILE
    bf = jnp.bfloat16
    rows = lambda width: pl.BlockSpec((ROW_TILE, width), lambda b, i: (b * nt + i, 0))
    const = lambda a: pl.BlockSpec(a.shape, lambda b, i: (0, 0))
    tab = pl.BlockSpec((ROW_TILE, LANE), lambda b, i: (i, 0))
    tok = lambda width: pl.BlockSpec((1, ROW_TILE, width), lambda b, i: (b, i, 0))
    tr = lambda height: pl.BlockSpec((1, height, ROW_TILE), lambda b, i: (b, 0, i))
    return pl.pallas_call(
        _front_body,
        out_shape=(
            jax.ShapeDtypeStruct((B * S, RG_WIDTH), bf),
            jax.ShapeDtypeStruct((B, NSA_WIDTH, S), bf),
            jax.ShapeDtypeStruct((B, S, KV_WIDTH), bf),
            jax.ShapeDtypeStruct((B, KV_WIDTH, S), bf),
            jax.ShapeDtypeStruct((B, S, KV_WIDTH), bf),
            jax.ShapeDtypeStruct((B, KV_WIDTH, S), bf),
            jax.ShapeDtypeStruct((B, LANE, S), jnp.float32),
            jax.ShapeDtypeStruct((B * S, 2 * KV_WIDTH), jnp.float32),
            jax.ShapeDtypeStruct((B * S, NSA_WIDTH), jnp.float32),
        ),
        grid=(B, nt),
        in_specs=[rows(D_MODEL), const(g), const(w),
                  const(cw), const(cb), const(wr), const(br), const(wi), const(bi), const(lam),
                  tab, tab, tab, const(qg), const(kgs), const(kgw), const(ind)],
        out_specs=(rows(RG_WIDTH), tr(NSA_WIDTH), tok(KV_WIDTH), tr(KV_WIDTH), tok(KV_WIDTH), tr(KV_WIDTH),
                   tr(LANE), rows(2 * KV_WIDTH), rows(NSA_WIDTH)),
        scratch_shapes=[
            pltpu.VMEM((SUB, RG_WIDTH), jnp.float32),
            pltpu.VMEM((SUB, RG_WIDTH), jnp.float32),
            pltpu.VMEM((RG_TILE, RG_WIDTH), jnp.float32),
            pltpu.VMEM((RG_TILE, RG_WIDTH), jnp.float32),
        ],
        compiler_params=_cparams(("arbitrary", "arbitrary")),
        name="front",
    )(x2, g, w, cw, cb, wr, br, wi, bi, lam, ra, rm, rp, qg, kgs, kgw, ind)


N_CHUNK_TOK = CMP_STRIDE


def _compress_one(src_ref, wa_ref, wb_ref, pea_ref, peb_ref, w2_ref, n_chunk):
    ha = jnp.zeros((n_chunk, 2 * CMP_HIDDEN), jnp.float32)
    hb = jnp.zeros((n_chunk, 2 * CMP_HIDDEN), jnp.float32)
    for j in range(N_CHUNK_TOK):
        xj = src_ref[pl.ds(j, n_chunk, stride=N_CHUNK_TOK), :]
        ha = ha + jnp.dot((xj + pea_ref[j]).astype(jnp.bfloat16), wa_ref[j],
                          preferred_element_type=jnp.float32)
        hb = hb + jnp.dot((xj + peb_ref[j]).astype(jnp.bfloat16), wb_ref[j],
                          preferred_element_type=jnp.float32)
    hid = ha + pltpu.roll(hb, n_chunk - 1, axis=0)
    act = hid * jax.nn.sigmoid(hid)
    return jnp.dot(act.astype(jnp.bfloat16), w2_ref[...], preferred_element_type=jnp.float32)


def _compress_body(kc_ref, vc_ref, wak_ref, wbk_ref, peak_ref, pebk_ref, w2k_ref,
                   wav_ref, wbv_ref, peav_ref, pebv_ref, w2v_ref,
                   kg_ref, ind_ref, ra_ref, rm_ref, rp_ref, ko_ref, vto_ref, *, n_chunk):
    kc = _compress_one(kc_ref, wak_ref, wbk_ref, peak_ref, pebk_ref, w2k_ref, n_chunk)
    ko_ref[0] = _norm_rope(kc, kg_ref[...], ind_ref[...], ra_ref[...], rm_ref[...],
                           rp_ref[...]).astype(jnp.bfloat16)
    vc = _compress_one(vc_ref, wav_ref, wbv_ref, peav_ref, pebv_ref, w2v_ref, n_chunk)
    vto_ref[0] = vc.T.astype(jnp.bfloat16)


def _compress(z, B, S, wk, wv, kg, ind, ra, rm, rp):
    n_chunk = S // N_CHUNK_TOK

    def full(a):
        nd = a.ndim
        return pl.BlockSpec(a.shape, lambda b: (0,) * nd)

    consts = list(wk) + list(wv) + [kg, ind, ra, rm, rp]
    return pl.pallas_call(
        functools.partial(_compress_body, n_chunk=n_chunk),
        out_shape=(
            jax.ShapeDtypeStruct((B, n_chunk, KV_WIDTH), jnp.bfloat16),
            jax.ShapeDtypeStruct((B, KV_WIDTH, n_chunk), jnp.bfloat16),
        ),
        grid=(B,),
        in_specs=[
            pl.BlockSpec((S, KV_WIDTH), lambda b: (b, 0)),
            pl.BlockSpec((S, KV_WIDTH), lambda b: (b, 1)),
        ] + [full(a) for a in consts],
        out_specs=(
            pl.BlockSpec((1, n_chunk, KV_WIDTH), lambda b: (b, 0, 0)),
            pl.BlockSpec((1, KV_WIDTH, n_chunk), lambda b: (b, 0, 0)),
        ),
        compiler_params=_cparams(("arbitrary",)),
        name="compress",
    )(z, z, *consts)


G_COL = HPG * Q_BLOCK
N_COL = N_KV * G_COL


def _group_dots(vts, p):
    p16 = p.astype(jnp.bfloat16)
    return jnp.concatenate(
        [jnp.dot(vt, p16[:, g * G_COL:(g + 1) * G_COL], preferred_element_type=jnp.float32)
         for g, vt in enumerate(vts)], axis=1)


MXU_COLS = 256
ONES_ROWS = 16
ACC_ROWS = HEAD_DIM + ONES_ROWS


def _with_ones(vt):
    return jnp.concatenate([vt, jnp.ones((ONES_ROWS, vt.shape[1]), vt.dtype)], axis=0)


N_CBLK = N_COL // MXU_COLS


def _cblk(c):
    return slice(c * MXU_COLS, (c + 1) * MXU_COLS)


def _tile_softmax(blocks, vts):
    vones = [_with_ones(vt) for vt in vts]
    ms, accs = [], []
    for c in range(N_CBLK):
        sc = blocks(c)
        m = jnp.max(sc, axis=0, keepdims=True)
        p = jnp.exp2(sc - m).astype(jnp.bfloat16)
        ms.append(m)
        accs.append(jnp.dot(vones[c * MXU_COLS // G_COL], p, preferred_element_type=jnp.float32))
    return jnp.concatenate(ms, axis=1), jnp.concatenate(accs, axis=1)


def _merge(state, tile):
    m0, acc0 = state
    m1, acc1 = tile
    m = jnp.maximum(m0, m1)
    return m, jnp.exp2(m0 - m) * acc0 + jnp.exp2(m1 - m) * acc1


def _finish(state):
    _, acc = state
    return acc[0:HEAD_DIM] * (1.0 / acc[HEAD_DIM:HEAD_DIM + 1])


def _nsa_body(qt_ref, kc_ref, vct_ref, ks_ref, vst_ref, kw_ref, vwt_ref, gt_ref, ng0_ref, ng1_ref, e_ref,
              y_ref, ps_ref, bias_ref, sa_ref, sb_ref, *, n_cmp_pad):
    qi = pl.program_id(1)
    q0 = qi * Q_BLOCK
    groups = range(N_KV)
    grows = lambda g: slice(g * HEAD_DIM, (g + 1) * HEAD_DIM)
    gcols = lambda g: slice(g * G_COL, (g + 1) * G_COL)

    qt = qt_ref[0]
    zero = jnp.zeros((HEAD_DIM, G_COL), jnp.bfloat16)
    qrows = []
    for g in groups:
        qcat = jnp.concatenate([qt[(g * HPG + h) * HEAD_DIM:(g * HPG + h + 1) * HEAD_DIM, :]
                                for h in range(HPG)], axis=1)
        qrows.append(jnp.concatenate([qcat if gg == g else zero for gg in groups], axis=1))
    qz = jnp.concatenate(qrows, axis=0)

    tl = lax.broadcasted_iota(jnp.int32, (1, N_COL), 1) % Q_BLOCK
    tcol = q0 + tl

    per = SEL_LEN // CMP_STRIDE
    n_sel_all = n_cmp_pad // per

    def compressed_and_topk(n_cmp):
        sc = jnp.dot(kc_ref[0, 0:n_cmp, :], qz, preferred_element_type=jnp.float32)
        crow = lax.broadcasted_iota(jnp.int32, (n_cmp, N_COL), 0)
        cvalid = crow <= (tcol - (CMP_LEN - 1)) // CMP_STRIDE
        scm = jnp.where(cvalid, sc, NEG)
        mc = jnp.max(scm, axis=0, keepdims=True)
        pc = jnp.exp2(scm - mc)
        lc = jnp.sum(pc, axis=0, keepdims=True)
        pc = pc * jnp.where(tcol >= CMP_LEN - 1, 1.0 / lc, 0.0)
        o_c = _group_dots([vct_ref[0, grows(g), 0:n_cmp] for g in groups], pc)

        n_sel = n_cmp // per
        tq = q0 + lax.broadcasted_iota(jnp.int32, (n_sel, Q_BLOCK), 1)
        bt = tq // SEL_LEN
        rowj = lax.broadcasted_iota(jnp.int32, (n_sel, Q_BLOCK), 0)
        forced = (rowj == 0) | (rowj == bt) | (rowj == bt - 1)
        valid = rowj <= bt
        rowf = rowj.astype(jnp.float32)
        cands = []
        for g in groups:
            psum = pc[:, g * G_COL:g * G_COL + Q_BLOCK]
            for h in range(1, HPG):
                psum = psum + pc[:, g * G_COL + h * Q_BLOCK:g * G_COL + (h + 1) * Q_BLOCK]
            nl = Q_BLOCK // LANE
            for l in range(nl):
                ps_ref[g * nl + l, 0:SUB, :] = jnp.zeros((SUB, LANE), jnp.float32)
                ps_ref[g * nl + l, SUB:SUB + n_cmp, :] = psum[:, l * LANE:(l + 1) * LANE]
            strided = lambda k: jnp.concatenate(
                [ps_ref[g * nl + l, pl.ds(SUB + k, n_sel, stride=per), :] for l in range(nl)], axis=1)
            imp = strided(0) + strided(1) + strided(2) + 0.5 * strided(3) + 0.5 * strided(-1)
            cands.append(jnp.where(valid & jnp.logical_not(forced), imp, -1.0))

        eligible = [c >= 0.0 for c in cands]
        for _ in range(SEL_TOPN - 3):
            for g in groups:
                mx = jnp.max(cands[g], axis=0, keepdims=True)
                first = jnp.min(jnp.where(cands[g] == mx, rowf, float(n_sel)), axis=0, keepdims=True)
                cands[g] = jnp.where(rowf == first, -1.0, cands[g])
        taken = [(forced & valid) | (eligible[g] & (cands[g] < 0.0)) for g in groups]
        bias_ref[0:n_sel, :] = jnp.concatenate([jnp.where(tk, 0.0, NEG) for tk in taken], axis=1)
        if n_sel < n_sel_all:
            bias_ref[n_sel:n_sel_all, :] = jnp.full((n_sel_all - n_sel, N_KV * Q_BLOCK), NEG, jnp.float32)
        bias_ref[n_sel_all:, :] = jnp.zeros((LANE, N_KV * Q_BLOCK), jnp.float32)
        return o_c

    n_var = 4
    quarter = n_cmp_pad // n_var
    variant = jnp.minimum((q0 + Q_BLOCK - CMP_LEN) // CMP_STRIDE // quarter, n_var - 1)
    o_c = lax.switch(variant, [functools.partial(compressed_and_topk, (v + 1) * quarter) for v in range(n_var)])

    e_mat = e_ref[...]

    def produce(dst_ref, t):
        koff = pl.multiple_of(t * SEL_TILE, SEL_TILE)
        boff = pl.multiple_of(t * SEL_BLOCKS, SEL_BLOCKS)
        brows = bias_ref[pl.ds(boff, LANE), :].astype(jnp.bfloat16)
        bcols = jnp.concatenate([brows[:, g * Q_BLOCK:(g + 1) * Q_BLOCK] for g in groups for _ in range(HPG)],
                                axis=1)
        rhs = jnp.concatenate([qz, bcols], axis=0)
        lhs = jnp.concatenate([ks_ref[0, pl.ds(koff, SEL_TILE), :], e_mat], axis=1)

        def block(c):
            dst_ref[:, _cblk(c)] = jnp.dot(lhs, rhs[:, _cblk(c)], preferred_element_type=jnp.float32)
        return block

    def sel_values(t):
        koff = pl.multiple_of(t * SEL_TILE, SEL_TILE)
        return [vst_ref[0, grows(g), pl.ds(koff, SEL_TILE)] for g in groups]

    def consume(s_ref, t, state, make_next=None, causal=False):
        srow = lax.broadcasted_iota(jnp.int32, (SEL_TILE, MXU_COLS), 0)
        limit = tcol - t * SEL_TILE

        def blocks(c):
            if make_next is not None:
                make_next(c)
            sc = s_ref[:, _cblk(c)]
            return jnp.where(srow <= limit[:, _cblk(c)], sc, NEG) if causal else sc
        return _merge(state, _tile_softmax(blocks, sel_values(t)))

    def pair(j, state):
        t = 2 * j
        state = consume(sa_ref, t, state, produce(sb_ref, t + 1))
        return consume(sb_ref, t + 1, state, produce(sa_ref, t + 2))

    n_open = q0 // SEL_TILE

    win_len = WINDOW + Q_BLOCK
    woff = pl.multiple_of(q0, Q_BLOCK)
    flag = jnp.where(lax.broadcasted_iota(jnp.int32, (LANE, N_COL), 0) == 0, NEG, 0.0)
    first_tile = produce(sa_ref, 0)
    for c in range(N_CBLK):
        first_tile(c)
    sw = jnp.dot(kw_ref[0, pl.ds(woff, win_len), :],
                 jnp.concatenate([qz, flag.astype(jnp.bfloat16)], axis=0),
                 preferred_element_type=jnp.float32)
    wrow = lax.broadcasted_iota(jnp.int32, (Q_BLOCK, N_COL), 0)
    sw = jnp.concatenate([jnp.where(wrow > tl, sw[0:Q_BLOCK], NEG),
                          sw[Q_BLOCK:WINDOW],
                          jnp.where(wrow <= tl, sw[WINDOW:], NEG)], axis=0)
    o_w = _finish(_tile_softmax(lambda c: sw[:, _cblk(c)],
                                [vwt_ref[0, grows(g), pl.ds(woff, win_len)] for g in groups]))

    state = (jnp.full((1, N_COL), NEG, jnp.float32), jnp.zeros((ACC_ROWS, N_COL), jnp.float32))
    n_pair = n_open // 2
    state = lax.fori_loop(0, n_pair // 2, lambda j, st: pair(2 * j + 1, pair(2 * j, st)), state)
    state = lax.fori_loop(2 * (n_pair // 2), n_pair, pair, state)
    t_next = 2 * (n_open // 2)

    def odd_tail(state):
        state = consume(sa_ref, t_next, state, produce(sb_ref, t_next + 1))
        return _finish(consume(sb_ref, t_next + 1, state, causal=True))

    def even_tail(state):
        return _finish(consume(sa_ref, t_next, state, causal=True))

    o_s = lax.cond(n_open % 2 == 1, odd_tail, even_tail, state)

    def gate_row(branch):
        return jnp.concatenate(
            [gt_ref[0, r * N_BRANCH + branch:r * N_BRANCH + branch + 1, :] for r in range(N_KV * HPG)], axis=1)

    yt = gate_row(0) * o_c + gate_row(1) * o_s + gate_row(2) * o_w
    halves = []
    for pr in range(N_KV * HPG // 2):
        two = jnp.concatenate([yt[:, (2 * pr) * Q_BLOCK:(2 * pr + 1) * Q_BLOCK],
                               yt[:, (2 * pr + 1) * Q_BLOCK:(2 * pr + 2) * Q_BLOCK]], axis=0)
        halves.append(two.T)
    y = jnp.concatenate(halves, axis=1)
    ng = jnp.concatenate([ng0_ref[...], ng1_ref[...]], axis=1)
    y_ref[...] = (y * (ng * jax.nn.sigmoid(ng))).astype(y_ref.dtype)


def _nsa(z, qt, kc, vct, ks, vst, kw, vwt, gt, B, S):
    nq = S // Q_BLOCK
    n_cmp_pad = S // CMP_STRIDE
    gw = HPG * HEAD_DIM
    bf = jnp.bfloat16
    e_mat = (jnp.arange(SEL_TILE)[:, None] // SEL_LEN == jnp.arange(LANE)[None, :]).astype(bf)
    lead = jnp.zeros((B, WINDOW, 2 * KV_WIDTH), bf).at[:, :, KV_WIDTH].set(1.0)
    kw_pad = jnp.concatenate([lead, jnp.concatenate([kw, jnp.zeros_like(kw)], axis=2)], axis=1)
    vwt_pad = jnp.pad(vwt, ((0, 0), (0, 0), (WINDOW, 0)))
    whole = lambda a: pl.BlockSpec((1,) + a.shape[1:], lambda b, i: (b, 0, 0))
    return pl.pallas_call(
        functools.partial(_nsa_body, n_cmp_pad=n_cmp_pad),
        out_shape=jax.ShapeDtypeStruct((B * S, NSA_WIDTH), bf),
        grid=(B, nq),
        in_specs=[
            pl.BlockSpec((1, NSA_WIDTH, Q_BLOCK), lambda b, i: (b, 0, i)),
            whole(kc), whole(vct), whole(ks), whole(vst), whole(kw_pad), whole(vwt_pad),
            pl.BlockSpec((1, LANE, Q_BLOCK), lambda b, i: (b, 0, i)),
            pl.BlockSpec((Q_BLOCK, gw), lambda b, i: (b * nq + i, 0)),
            pl.BlockSpec((Q_BLOCK, gw), lambda b, i: (b * nq + i, 1)),
            pl.BlockSpec((SEL_TILE, LANE), lambda b, i: (0, 0)),
        ],
        out_specs=pl.BlockSpec((Q_BLOCK, NSA_WIDTH), lambda b, i: (b * nq + i, 0)),
        scratch_shapes=[
            pltpu.VMEM((N_KV * Q_BLOCK // LANE, SUB + n_cmp_pad, LANE), jnp.float32),
            pltpu.VMEM((S // SEL_LEN + LANE, N_KV * Q_BLOCK), jnp.float32),
            pltpu.VMEM((SEL_TILE, N_COL), jnp.float32),
            pltpu.VMEM((SEL_TILE, N_COL), jnp.float32),
        ],
        compiler_params=_cparams(("arbitrary", "arbitrary")),
        name="nsa",
    )(qt, kc, vct, ks, vst, kw_pad, vwt_pad, gt, z, z, e_mat)


def _out_proj_body(x_ref, ya_ref, yb_ref, wa_ref, wb_ref, o_ref):
    o_ref[...] = (x_ref[...]
                  + jnp.dot(ya_ref[...], wa_ref[...], preferred_element_type=jnp.float32)
                  + jnp.dot(yb_ref[...], wb_ref[...], preferred_element_type=jnp.float32))


def _out_proj(x2, ya, yb, wa, wb):
    rows = x2.shape[0]
    return pl.pallas_call(
        _out_proj_body,
        out_shape=jax.ShapeDtypeStruct((rows, D_MODEL), jnp.float32),
        grid=(rows // ROW_TILE,),
        in_specs=[
            pl.BlockSpec((ROW_TILE, D_MODEL), lambda i: (i, 0)),
            pl.BlockSpec((ROW_TILE, RG_WIDTH), lambda i: (i, 0)),
            pl.BlockSpec((ROW_TILE, NSA_WIDTH), lambda i: (i, 0)),
            pl.BlockSpec((RG_WIDTH, D_MODEL), lambda i: (0, 0)),
            pl.BlockSpec((NSA_WIDTH, D_MODEL), lambda i: (0, 0)),
        ],
        out_specs=pl.BlockSpec((ROW_TILE, D_MODEL), lambda i: (i, 0)),
        compiler_params=_cparams(("arbitrary",)),
        name="out_proj",
    )(x2, ya, yb, wa, wb)


def _rope_tables(pos):
    half = ROPE_DIM // 2
    inv = ROPE_THETA ** (-jnp.arange(half, dtype=jnp.float32) / half)
    ang = pos.astype(jnp.float32)[:, None] * inv[None, :]
    cos, sin = jnp.cos(ang), jnp.sin(ang)
    n = pos.shape[0]
    rest = HEAD_DIM - ROPE_DIM
    ra = jnp.concatenate([cos, cos, jnp.ones((n, rest), jnp.float32)], axis=1)
    rm = jnp.concatenate([-sin, jnp.zeros((n, half + rest), jnp.float32)], axis=1)
    rp = jnp.concatenate([jnp.zeros((n, half), jnp.float32), sin, jnp.zeros((n, rest), jnp.float32)], axis=1)
    return tuple(jnp.tile(t, (1, LANE // HEAD_DIM)) for t in (ra, rm, rp))


def _block_diag(blocks):
    n, r, c = blocks.shape
    eye = jnp.eye(n, dtype=blocks.dtype)
    return (eye[:, None, :, None] * blocks[:, :, None, :]).reshape(n * r, n * c)


def _compress_params(pe, w1, w2):
    bf = jnp.bfloat16
    w1t = w1.reshape(2, N_CHUNK_TOK, HEAD_DIM, CMP_HIDDEN)
    dup = lambda w: jax.vmap(lambda m: _block_diag(jnp.stack([m] * N_KV)))(w)
    wa, wb = dup(w1t[0]).astype(bf), dup(w1t[1]).astype(bf)
    pet = jnp.tile(pe.reshape(2, N_CHUNK_TOK, 1, HEAD_DIM), (1, 1, 1, N_KV))
    w2d = _block_diag(jnp.stack([w2] * N_KV)).astype(bf)
    return wa, wb, pet[0], pet[1], w2d


def _layer(x2, B, S, norm_g, w_in, conv_w, conv_b, rg_wr, rg_br, rg_wi, rg_bi, rg_lam,
           q_g, k_g, pe_k, w1_k, w2_k, pe_v, w1_v, w2_v, w_out, tabs_tok, tabs_cmp, ind):
    bf = jnp.bfloat16
    w_in_p = jnp.pad(w_in, ((0, 0), (0, N_PAD - N_IN))).astype(bf)
    tile2 = lambda v: jnp.tile(v, LANE // HEAD_DIM)[None, :]
    y_a, qt, ks, vst, kw, vwt, gt, kcvc, ng = _front(
        x2, B, S, norm_g[None, :], w_in_p,
        conv_w, conv_b[None, :], _block_diag(rg_wr).astype(bf), rg_br[None, :],
        _block_diag(rg_wi).astype(bf), rg_bi[None, :], rg_lam[None, :],
        *tabs_tok, tile2(q_g), tile2(k_g[1]), tile2(k_g[2]), ind)
    kc, vct = _compress(kcvc, B, S, _compress_params(pe_k, w1_k, w2_k), _compress_params(pe_v, w1_v, w2_v),
                        tile2(k_g[0]), ind, *tabs_cmp)
    y_b = _nsa(ng, qt, kc, vct, ks, vst, kw, vwt, gt, B, S)

    w_out16 = w_out.astype(bf)
    return _out_proj(x2, y_a, y_b, w_out16[:RG_WIDTH], w_out16[RG_WIDTH:])


def kernel(x, norm_g, w_in, conv_w, conv_b, rg_wr, rg_br, rg_wi, rg_bi, rg_lambda, q_norm_g, k_norm_g,
           cmp_pe_k, cmp_w1_k, cmp_w2_k, cmp_pe_v, cmp_w1_v, cmp_w2_v, w_out):
    B, S, D = x.shape
    assert D == D_MODEL and S % ROW_TILE == 0 and S % SEL_TILE == 0 and S >= WINDOW
    depth = norm_g.shape[0]
    tabs_tok = _rope_tables(jnp.arange(S))
    tabs_cmp = _rope_tables(jnp.arange(S // CMP_STRIDE) * CMP_STRIDE + (CMP_LEN - 1))
    ind = _block_diag(jnp.ones((LANE // HEAD_DIM, HEAD_DIM, HEAD_DIM), jnp.bfloat16))
    x2 = x.reshape(B * S, D)
    for l in range(depth):
        x2 = _layer(x2, B, S, norm_g[l], w_in[l], conv_w[l], conv_b[l], rg_wr[l], rg_br[l], rg_wi[l],
                    rg_bi[l], rg_lambda[l], q_norm_g[l], k_norm_g[l], cmp_pe_k[l], cmp_w1_k[l],
                    cmp_w2_k[l], cmp_pe_v[l], cmp_w1_v[l], cmp_w2_v[l], w_out[l], tabs_tok, tabs_cmp, ind)
    return x2.reshape(B, S, D)
```

```python
import functools
import math

import jax
import jax.numpy as jnp
from jax import lax
from jax.experimental import pallas as pl
from jax.experimental.pallas import tpu as pltpu

D_MODEL = 1024
RG_WIDTH = 512
RG_BLOCKS = 8
RG_BLOCK = 64
CONV_W = 4
RG_C = 8.0
NSA_WIDTH = 512
HEAD_DIM = 64
N_KV = 2
HPG = 4
KV_WIDTH = N_KV * HEAD_DIM
ROPE_DIM = 16
ROPE_THETA = 500000.0
CMP_LEN = 32
CMP_STRIDE = 16
CMP_HIDDEN = 128
SEL_LEN = 64
SEL_TOPN = 16
WINDOW = 512
N_BRANCH = 3
Q_BLOCK = 256
EPS = 1e-6
NEG = -1e30
N_IN = 2 * RG_WIDTH + 2 * NSA_WIDTH + 6 * KV_WIDTH + N_BRANCH * 8
LANE = 128
MXU_COLS = 256
N_PAD = ((N_IN + LANE - 1) // LANE) * LANE

COL_Q = 2 * RG_WIDTH
COL_KC = COL_Q + NSA_WIDTH
COL_VC = COL_KC + KV_WIDTH
COL_KS = COL_VC + KV_WIDTH
COL_VS = COL_KS + KV_WIDTH
COL_KW = COL_VS + KV_WIDTH
COL_VW = COL_KW + KV_WIDTH
COL_NG = COL_VW + KV_WIDTH
COL_BG = COL_NG + NSA_WIDTH

VMEM_LIMIT = 56 * 1024 * 1024

ROW_TILE = 512
OUT_TILE = 2048
SEL_TILE = 512
SEL_BLOCKS = SEL_TILE // SEL_LEN
LOG2E = 1.4426950408889634


def _cparams(sem):
    return pltpu.CompilerParams(dimension_semantics=sem, vmem_limit_bytes=VMEM_LIMIT)


def _head_mean_sq(x, ind):
    sq = x * x
    hi = sq.astype(jnp.bfloat16)
    lo = (sq - hi.astype(jnp.float32)).astype(jnp.bfloat16)
    tot = (jnp.dot(hi, ind, preferred_element_type=jnp.float32)
           + jnp.dot(lo, ind, preferred_element_type=jnp.float32))
    return tot * (1.0 / HEAD_DIM)


def _norm_rope(x, gain, ind, ra, rm, rp):
    width = x.shape[-1]
    y = (x * lax.rsqrt(_head_mean_sq(x, ind) + EPS)) * gain
    up = pltpu.roll(y, width - ROPE_DIM // 2, axis=1)
    dn = pltpu.roll(y, ROPE_DIM // 2, axis=1)
    return y * ra + up * rm + dn * rp


def _prep_q(zq, first, tabs, qg_ref, ind_ref, qt_ref):
    scale = LOG2E / math.sqrt(HEAD_DIM)
    for c in range(zq.shape[1] // LANE):
        qn = _norm_rope(zq[:, c * LANE:(c + 1) * LANE], qg_ref[...], ind_ref[...], *tabs) * scale
        qt_ref[0, (first + c) * LANE:(first + c + 1) * LANE, :] = qn.T.astype(jnp.bfloat16)


def _prep_kv(zkv, tabs, kg_ref, ind_ref, k_ref, vt_ref):
    k_ref[0] = _norm_rope(zkv[:, :KV_WIDTH], kg_ref[...], ind_ref[...], *tabs).astype(jnp.bfloat16)
    vt_ref[0] = zkv[:, KV_WIDTH:].T.astype(jnp.bfloat16)


RG_TILE = 512
RG_CHUNKS = 16
SUB = 8


def _block_diag_dot(x, w_ref):
    n = w_ref.shape[0] // MXU_COLS
    return jnp.concatenate(
        [jnp.dot(x[:, c * MXU_COLS:(c + 1) * MXU_COLS],
                 w_ref[c * MXU_COLS:(c + 1) * MXU_COLS, c * MXU_COLS:(c + 1) * MXU_COLS],
                 preferred_element_type=jnp.float32) for c in range(n)], axis=1)


def _rglru_tile(x, gate, cw_ref, cb_ref, wr_ref, br_ref, wi_ref, bi_ref, lam_ref,
                y_ref, tail_ref, carry_ref, a_ref, b_ref):
    tail = tail_ref[...]
    row8 = lax.broadcasted_iota(jnp.int32, (SUB, RG_WIDTH), 0)
    conv = x * cw_ref[CONV_W - 1:CONV_W, :] + cb_ref[...]
    for s in range(1, CONV_W):
        xs = pltpu.roll(x, s, axis=0)
        first = jnp.where(row8 < s, pltpu.roll(tail, s, axis=0), xs[0:SUB])
        xs = jnp.concatenate([first, xs[SUB:]], axis=0)
        conv = conv + xs * cw_ref[CONV_W - 1 - s:CONV_W - s, :]
    tail_ref[...] = x[RG_TILE - SUB:]
    yield

    softplus = jax.nn.softplus(-lam_ref[...])
    chunk = RG_TILE // RG_CHUNKS
    grp = (chunk // SUB, SUB, RG_WIDTH)
    row = lax.broadcasted_iota(jnp.int32, grp, 1)
    for c in range(RG_CHUNKS):
        rows = slice(c * chunk, (c + 1) * chunk)
        cv = conv[rows]
        cb16 = cv.astype(jnp.bfloat16)
        r = jax.nn.sigmoid(_block_diag_dot(cb16, wr_ref) + br_ref[...])
        ig = jax.nn.sigmoid(_block_diag_dot(cb16, wi_ref) + bi_ref[...])
        log_a = (-RG_C * r) * softplus
        a = jnp.exp(log_a)
        u = jnp.sqrt(-jnp.tanh(log_a) * (a * a + 1.0)) * (ig * cv)
        a, u = a.reshape(grp), u.reshape(grp)
        for d in (1, 2, 4):
            keep = row >= d
            a_s = jnp.where(keep, pltpu.roll(a, d, axis=1), 1.0)
            u_s = jnp.where(keep, pltpu.roll(u, d, axis=1), 0.0)
            u = a * u_s + u
            a = a * a_s
        a_ref[rows, :] = a.reshape(chunk, RG_WIDTH)
        b_ref[rows, :] = u.reshape(chunk, RG_WIDTH)
        yield

    h = carry_ref[...]
    for c in range(RG_CHUNKS):
        for k in range(c * chunk // SUB, (c + 1) * chunk // SUB):
            grows = slice(k * SUB, (k + 1) * SUB)
            hk = a_ref[grows, :] * h + b_ref[grows, :]
            b_ref[grows, :] = hk
            h = jnp.broadcast_to(hk[SUB - 1:SUB, :], (SUB, RG_WIDTH))
        yield
    carry_ref[...] = h

    for c in range(RG_CHUNKS):
        rows = slice(c * chunk, (c + 1) * chunk)
        gt = gate[rows]
        y_ref[rows, :] = (b_ref[rows, :] * (gt * jax.nn.sigmoid(gt))).astype(y_ref.dtype)
        yield


def _front_body(x_ref, g_ref, w_ref,
                cw_ref, cb_ref, wr_ref, br_ref, wi_ref, bi_ref, lam_ref,
                ra_ref, rm_ref, rp_ref, qg_ref, kgs_ref, kgw_ref, ind_ref,
                ya_ref, qt_ref, kso_ref, vst_ref, kwo_ref, vwt_ref, gt_ref, kcvc_ref, ng_ref,
                tail_ref, carry_ref, a_ref, b_ref):
    @pl.when(pl.program_id(1) == 0)
    def _():
        tail_ref[...] = jnp.zeros_like(tail_ref)
        carry_ref[...] = jnp.zeros_like(carry_ref)

    x = x_ref[...]
    ms = jnp.mean(x * x, axis=-1, keepdims=True)
    h = ((x * lax.rsqrt(ms + EPS)) * g_ref[...]).astype(jnp.bfloat16)
    proj = lambda c0, c1: jnp.dot(h, w_ref[:, c0:c1], preferred_element_type=jnp.float32)

    z_rg = proj(0, COL_Q)
    mixer = _rglru_tile(z_rg[:, :RG_WIDTH], z_rg[:, RG_WIDTH:], cw_ref, cb_ref, wr_ref, br_ref, wi_ref,
                        bi_ref, lam_ref, ya_ref, tail_ref, carry_ref, a_ref, b_ref)
    tabs = (ra_ref[...], rm_ref[...], rp_ref[...])
    half = 2 * KV_WIDTH
    z = {}

    def project(name, c0, c1):
        z[name] = proj(c0, c1)

    def store(ref, c0, c1):
        ref[...] = proj(c0, c1)

    def gates():
        gt_ref[0] = jax.nn.sigmoid(proj(COL_BG, N_PAD)).T

    items = [
        functools.partial(project, "q0", COL_Q, COL_Q + half),
        functools.partial(project, "q1", COL_Q + half, COL_KC),
        functools.partial(project, "sel", COL_KS, COL_KW),
        functools.partial(project, "win", COL_KW, COL_NG),
        functools.partial(store, kcvc_ref, COL_KC, COL_KS),
        lambda: _prep_q(z["q0"], 0, tabs, qg_ref, ind_ref, qt_ref),
        functools.partial(store, ng_ref.at[:, 0:half], COL_NG, COL_NG + half),
        lambda: _prep_q(z["q1"], half // LANE, tabs, qg_ref, ind_ref, qt_ref),
        functools.partial(store, ng_ref.at[:, half:], COL_NG + half, COL_BG),
        lambda: _prep_kv(z["sel"], tabs, kgs_ref, ind_ref, kso_ref, vst_ref),
        gates,
        lambda: _prep_kv(z["win"], tabs, kgw_ref, ind_ref, kwo_ref, vwt_ref),
    ]
    every = (1 + 3 * RG_CHUNKS) // len(items)
    for i, _ in enumerate(mixer):
        if items and i % every == 0:
            items.pop(0)()
    while items:
        items.pop(0)()


def _front(x2, B, S, g, w, cw, cb, wr, br, wi, bi, lam, ra, rm, rp, qg, kgs, kgw, ind):
    nt = S // ROW_TILE
    bf = jnp.bfloat16
    rows = lambda width: pl.BlockSpec((ROW_TILE, width), lambda b, i: (b * nt + i, 0))
    const = lambda a: pl.BlockSpec(a.shape, lambda b, i: (0, 0))
    tab = pl.BlockSpec((ROW_TILE, LANE), lambda b, i: (i, 0))
    tok = lambda width: pl.BlockSpec((1, ROW_TILE, width), lambda b, i: (b, i, 0))
    tr = lambda height: pl.BlockSpec((1, height, ROW_TILE), lambda b, i: (b, 0, i))
    return pl.pallas_call(
        _front_body,
        out_shape=(
            jax.ShapeDtypeStruct((B * S, RG_WIDTH), bf),
            jax.ShapeDtypeStruct((B, NSA_WIDTH, S), bf),
            jax.ShapeDtypeStruct((B, S, KV_WIDTH), bf),
            jax.ShapeDtypeStruct((B, KV_WIDTH, S), bf),
            jax.ShapeDtypeStruct((B, S, KV_WIDTH), bf),
            jax.ShapeDtypeStruct((B, KV_WIDTH, S), bf),
            jax.ShapeDtypeStruct((B, LANE, S), jnp.float32),
            jax.ShapeDtypeStruct((B * S, 2 * KV_WIDTH), jnp.float32),
            jax.ShapeDtypeStruct((B * S, NSA_WIDTH), jnp.float32),
        ),
        grid=(B, nt),
        in_specs=[rows(D_MODEL), const(g), const(w),
                  const(cw), const(cb), const(wr), const(br), const(wi), const(bi), const(lam),
                  tab, tab, tab, const(qg), const(kgs), const(kgw), const(ind)],
        out_specs=(rows(RG_WIDTH), tr(NSA_WIDTH), tok(KV_WIDTH), tr(KV_WIDTH), tok(KV_WIDTH), tr(KV_WIDTH),
                   tr(LANE), rows(2 * KV_WIDTH), rows(NSA_WIDTH)),
        scratch_shapes=[
            pltpu.VMEM((SUB, RG_WIDTH), jnp.float32),
            pltpu.VMEM((SUB, RG_WIDTH), jnp.float32),
            pltpu.VMEM((RG_TILE, RG_WIDTH), jnp.float32),
            pltpu.VMEM((RG_TILE, RG_WIDTH), jnp.float32),
        ],
        compiler_params=_cparams(("arbitrary", "arbitrary")),
        name="front",
    )(x2, g, w, cw, cb, wr, br, wi, bi, lam, ra, rm, rp, qg, kgs, kgw, ind)


N_CHUNK_TOK = CMP_STRIDE


def _compress_one(src_ref, wa_ref, wb_ref, pea_ref, peb_ref, w2_ref, n_chunk):
    ha = jnp.zeros((n_chunk, 2 * CMP_HIDDEN), jnp.float32)
    hb = jnp.zeros((n_chunk, 2 * CMP_HIDDEN), jnp.float32)
    for j in range(N_CHUNK_TOK):
        xj = src_ref[pl.ds(j, n_chunk, stride=N_CHUNK_TOK), :]
        ha = ha + jnp.dot((xj + pea_ref[j]).astype(jnp.bfloat16), wa_ref[j],
                          preferred_element_type=jnp.float32)
        hb = hb + jnp.dot((xj + peb_ref[j]).astype(jnp.bfloat16), wb_ref[j],
                          preferred_element_type=jnp.float32)
    hid = ha + pltpu.roll(hb, n_chunk - 1, axis=0)
    act = hid * jax.nn.sigmoid(hid)
    return jnp.dot(act.astype(jnp.bfloat16), w2_ref[...], preferred_element_type=jnp.float32)


def _compress_body(kc_ref, vc_ref, wak_ref, wbk_ref, peak_ref, pebk_ref, w2k_ref,
                   wav_ref, wbv_ref, peav_ref, pebv_ref, w2v_ref,
                   kg_ref, ind_ref, ra_ref, rm_ref, rp_ref, ko_ref, vto_ref, *, n_chunk):
    kc = _compress_one(kc_ref, wak_ref, wbk_ref, peak_ref, pebk_ref, w2k_ref, n_chunk)
    ko_ref[0] = _norm_rope(kc, kg_ref[...], ind_ref[...], ra_ref[...], rm_ref[...],
                           rp_ref[...]).astype(jnp.bfloat16)
    vc = _compress_one(vc_ref, wav_ref, wbv_ref, peav_ref, pebv_ref, w2v_ref, n_chunk)
    vto_ref[0] = vc.T.astype(jnp.bfloat16)


def _compress(z, B, S, wk, wv, kg, ind, ra, rm, rp):
    n_chunk = S // N_CHUNK_TOK

    def full(a):
        nd = a.ndim
        return pl.BlockSpec(a.shape, lambda b: (0,) * nd)

    consts = list(wk) + list(wv) + [kg, ind, ra, rm, rp]
    return pl.pallas_call(
        functools.partial(_compress_body, n_chunk=n_chunk),
        out_shape=(
            jax.ShapeDtypeStruct((B, n_chunk, KV_WIDTH), jnp.bfloat16),
            jax.ShapeDtypeStruct((B, KV_WIDTH, n_chunk), jnp.bfloat16),
        ),
        grid=(B,),
        in_specs=[
            pl.BlockSpec((S, KV_WIDTH), lambda b: (b, 0)),
            pl.BlockSpec((S, KV_WIDTH), lambda b: (b, 1)),
        ] + [full(a) for a in consts],
        out_specs=(
            pl.BlockSpec((1, n_chunk, KV_WIDTH), lambda b: (b, 0, 0)),
            pl.BlockSpec((1, KV_WIDTH, n_chunk), lambda b: (b, 0, 0)),
        ),
        compiler_params=_cparams(("arbitrary",)),
        name="compress",
    )(z, z, *consts)


G_COL = HPG * Q_BLOCK
N_COL = N_KV * G_COL


def _group_dots(vts, p):
    p16 = p.astype(jnp.bfloat16)
    return jnp.concatenate(
        [jnp.dot(vt, p16[:, g * G_COL:(g + 1) * G_COL], preferred_element_type=jnp.float32)
         for g, vt in enumerate(vts)], axis=1)


ONES_ROWS = 16
ACC_ROWS = HEAD_DIM + ONES_ROWS


def _with_ones(vt):
    return jnp.concatenate([vt, jnp.ones((ONES_ROWS, vt.shape[1]), vt.dtype)], axis=0)


N_CBLK = N_COL // MXU_COLS


def _cblk(c):
    return slice(c * MXU_COLS, (c + 1) * MXU_COLS)


def _tile_softmax(blocks, vts):
    vones = [_with_ones(vt) for vt in vts]
    ms, accs = [], []
    for c in range(N_CBLK):
        sc = blocks(c)
        m = jnp.max(sc, axis=0, keepdims=True)
        p = jnp.exp2(sc - m).astype(jnp.bfloat16)
        ms.append(m)
        accs.append(jnp.dot(vones[c * MXU_COLS // G_COL], p, preferred_element_type=jnp.float32))
    return jnp.concatenate(ms, axis=1), jnp.concatenate(accs, axis=1)


def _merge(state, tile):
    m0, acc0 = state
    m1, acc1 = tile
    m = jnp.maximum(m0, m1)
    return m, jnp.exp2(m0 - m) * acc0 + jnp.exp2(m1 - m) * acc1


def _finish(state):
    _, acc = state
    return acc[0:HEAD_DIM] * (1.0 / acc[HEAD_DIM:HEAD_DIM + 1])


def _nsa_body(qt_ref, kc_ref, vct_ref, ks_ref, vst_ref, kw_ref, vwt_ref, gt_ref, ng0_ref, ng1_ref, e_ref,
              y_ref, ps_ref, bias_ref, sa_ref, sb_ref, *, n_cmp_pad):
    qi = pl.program_id(1)
    q0 = qi * Q_BLOCK
    groups = range(N_KV)
    grows = lambda g: slice(g * HEAD_DIM, (g + 1) * HEAD_DIM)

    qt = qt_ref[0]
    zero = jnp.zeros((HEAD_DIM, G_COL), jnp.bfloat16)
    qrows = []
    for g in groups:
        qcat = jnp.concatenate([qt[(g * HPG + h) * HEAD_DIM:(g * HPG + h + 1) * HEAD_DIM, :]
                                for h in range(HPG)], axis=1)
        qrows.append(jnp.concatenate([qcat if gg == g else zero for gg in groups], axis=1))
    qz = jnp.concatenate(qrows, axis=0)

    tl = lax.broadcasted_iota(jnp.int32, (1, N_COL), 1) % Q_BLOCK
    tcol = q0 + tl

    per = SEL_LEN // CMP_STRIDE
    n_sel_all = n_cmp_pad // per

    def compressed_and_topk(n_cmp):
        sc = jnp.dot(kc_ref[0, 0:n_cmp, :], qz, preferred_element_type=jnp.float32)
        crow = lax.broadcasted_iota(jnp.int32, (n_cmp, N_COL), 0)
        cvalid = crow <= (tcol - (CMP_LEN - 1)) // CMP_STRIDE
        scm = jnp.where(cvalid, sc, NEG)
        mc = jnp.max(scm, axis=0, keepdims=True)
        pc = jnp.exp2(scm - mc)
        lc = jnp.sum(pc, axis=0, keepdims=True)
        pc = pc * jnp.where(tcol >= CMP_LEN - 1, 1.0 / lc, 0.0)
        o_c = _group_dots([vct_ref[0, grows(g), 0:n_cmp] for g in groups], pc)

        n_sel = n_cmp // per
        tq = q0 + lax.broadcasted_iota(jnp.int32, (n_sel, Q_BLOCK), 1)
        bt = tq // SEL_LEN
        rowj = lax.broadcasted_iota(jnp.int32, (n_sel, Q_BLOCK), 0)
        forced = (rowj == 0) | (rowj == bt) | (rowj == bt - 1)
        valid = rowj <= bt
        rowf = rowj.astype(jnp.float32)
        cands = []
        for g in groups:
            psum = pc[:, g * G_COL:g * G_COL + Q_BLOCK]
            for h in range(1, HPG):
                psum = psum + pc[:, g * G_COL + h * Q_BLOCK:g * G_COL + (h + 1) * Q_BLOCK]
            nl = Q_BLOCK // LANE
            for l in range(nl):
                ps_ref[g * nl + l, 0:SUB, :] = jnp.zeros((SUB, LANE), jnp.float32)
                ps_ref[g * nl + l, SUB:SUB + n_cmp, :] = psum[:, l * LANE:(l + 1) * LANE]
            strided = lambda k: jnp.concatenate(
                [ps_ref[g * nl + l, pl.ds(SUB + k, n_sel, stride=per), :] for l in range(nl)], axis=1)
            imp = strided(0) + strided(1) + strided(2) + 0.5 * strided(3) + 0.5 * strided(-1)
            cands.append(jnp.where(valid & jnp.logical_not(forced), imp, -1.0))

        eligible = [c >= 0.0 for c in cands]
        for _ in range(SEL_TOPN - 3):
            for g in groups:
                mx = jnp.max(cands[g], axis=0, keepdims=True)
                first = jnp.min(jnp.where(cands[g] == mx, rowf, float(n_sel)), axis=0, keepdims=True)
                cands[g] = jnp.where(rowf == first, -1.0, cands[g])
        taken = [(forced & valid) | (eligible[g] & (cands[g] < 0.0)) for g in groups]
        bias_ref[0:n_sel, :] = jnp.concatenate([jnp.where(tk, 0.0, NEG) for tk in taken], axis=1)
        if n_sel < n_sel_all:
            bias_ref[n_sel:n_sel_all, :] = jnp.full((n_sel_all - n_sel, N_KV * Q_BLOCK), NEG, jnp.float32)
        bias_ref[n_sel_all:, :] = jnp.zeros((LANE, N_KV * Q_BLOCK), jnp.float32)
        return o_c

    n_var = 4
    quarter = n_cmp_pad // n_var
    variant = jnp.minimum((q0 + Q_BLOCK - CMP_LEN) // CMP_STRIDE // quarter, n_var - 1)
    o_c = lax.switch(variant, [functools.partial(compressed_and_topk, (v + 1) * quarter) for v in range(n_var)])

    e_mat = e_ref[...]

    def produce(dst_ref, t):
        koff = pl.multiple_of(t * SEL_TILE, SEL_TILE)
        boff = pl.multiple_of(t * SEL_BLOCKS, SEL_BLOCKS)
        brows = bias_ref[pl.ds(boff, LANE), :].astype(jnp.bfloat16)
        bcols = jnp.concatenate([brows[:, g * Q_BLOCK:(g + 1) * Q_BLOCK] for g in groups for _ in range(HPG)],
                                axis=1)
        rhs = jnp.concatenate([qz, bcols], axis=0)
        lhs = jnp.concatenate([ks_ref[0, pl.ds(koff, SEL_TILE), :], e_mat], axis=1)

        def block(c):
            dst_ref[:, _cblk(c)] = jnp.dot(lhs, rhs[:, _cblk(c)], preferred_element_type=jnp.float32)
        return block

    def sel_values(t):
        koff = pl.multiple_of(t * SEL_TILE, SEL_TILE)
        return [vst_ref[0, grows(g), pl.ds(koff, SEL_TILE)] for g in groups]

    def consume(s_ref, t, state, make_next=None, causal=False):
        srow = lax.broadcasted_iota(jnp.int32, (SEL_TILE, MXU_COLS), 0)
        limit = tcol - t * SEL_TILE

        def blocks(c):
            if make_next is not None:
                make_next(c)
            sc = s_ref[:, _cblk(c)]
            return jnp.where(srow <= limit[:, _cblk(c)], sc, NEG) if causal else sc
        return _merge(state, _tile_softmax(blocks, sel_values(t)))

    def pair(j, state):
        t = 2 * j
        state = consume(sa_ref, t, state, produce(sb_ref, t + 1))
        return consume(sb_ref, t + 1, state, produce(sa_ref, t + 2))

    n_open = q0 // SEL_TILE

    win_len = WINDOW + Q_BLOCK
    woff = pl.multiple_of(q0, Q_BLOCK)
    flag = jnp.where(lax.broadcasted_iota(jnp.int32, (LANE, N_COL), 0) == 0, NEG, 0.0)
    first_tile = produce(sa_ref, 0)
    for c in range(N_CBLK):
        first_tile(c)
    sw = jnp.dot(kw_ref[0, pl.ds(woff, win_len), :],
                 jnp.concatenate([qz, flag.astype(jnp.bfloat16)], axis=0),
                 preferred_element_type=jnp.float32)
    wrow = lax.broadcasted_iota(jnp.int32, (Q_BLOCK, N_COL), 0)
    sw = jnp.concatenate([jnp.where(wrow > tl, sw[0:Q_BLOCK], NEG)]
                         + ([sw[Q_BLOCK:WINDOW]] if Q_BLOCK < WINDOW else [])
                         + [jnp.where(wrow <= tl, sw[WINDOW:], NEG)], axis=0)
    o_w = _finish(_tile_softmax(lambda c: sw[:, _cblk(c)],
                                [vwt_ref[0, grows(g), pl.ds(woff, win_len)] for g in groups]))

    state = (jnp.full((1, N_COL), NEG, jnp.float32), jnp.zeros((ACC_ROWS, N_COL), jnp.float32))
    n_pair = n_open // 2
    state = lax.fori_loop(0, n_pair // 2, lambda j, st: pair(2 * j + 1, pair(2 * j, st)), state)
    state = lax.fori_loop(2 * (n_pair // 2), n_pair, pair, state)
    t_next = 2 * (n_open // 2)

    def odd_tail(state):
        state = consume(sa_ref, t_next, state, produce(sb_ref, t_next + 1))
        return _finish(consume(sb_ref, t_next + 1, state, causal=True))

    def even_tail(state):
        return _finish(consume(sa_ref, t_next, state, causal=True))

    o_s = lax.cond(n_open % 2 == 1, odd_tail, even_tail, state)

    def gate_row(branch):
        return jnp.concatenate(
            [gt_ref[0, r * N_BRANCH + branch:r * N_BRANCH + branch + 1, :] for r in range(N_KV * HPG)], axis=1)

    yt = gate_row(0) * o_c + gate_row(1) * o_s + gate_row(2) * o_w
    halves = []
    for pr in range(N_KV * HPG // 2):
        two = jnp.concatenate([yt[:, (2 * pr) * Q_BLOCK:(2 * pr + 1) * Q_BLOCK],
                               yt[:, (2 * pr + 1) * Q_BLOCK:(2 * pr + 2) * Q_BLOCK]], axis=0)
        halves.append(two.T)
    y = jnp.concatenate(halves, axis=1)
    ng = jnp.concatenate([ng0_ref[...], ng1_ref[...]], axis=1)
    y_ref[...] = (y * (ng * jax.nn.sigmoid(ng))).astype(y_ref.dtype)


def _nsa(z, qt, kc, vct, ks, vst, kw, vwt, gt, B, S):
    nq = S // Q_BLOCK
    n_cmp_pad = S // CMP_STRIDE
    gw = HPG * HEAD_DIM
    bf = jnp.bfloat16
    e_mat = (jnp.arange(SEL_TILE)[:, None] // SEL_LEN == jnp.arange(LANE)[None, :]).astype(bf)
    lead = jnp.zeros((B, WINDOW, 2 * KV_WIDTH), bf).at[:, :, KV_WIDTH].set(1.0)
    kw_pad = jnp.concatenate([lead, jnp.concatenate([kw, jnp.zeros_like(kw)], axis=2)], axis=1)
    vwt_pad = jnp.pad(vwt, ((0, 0), (0, 0), (WINDOW, 0)))
    whole = lambda a: pl.BlockSpec((1,) + a.shape[1:], lambda b, i: (b, 0, 0))
    return pl.pallas_call(
        functools.partial(_nsa_body, n_cmp_pad=n_cmp_pad),
        out_shape=jax.ShapeDtypeStruct((B * S, NSA_WIDTH), bf),
        grid=(B, nq),
        in_specs=[
            pl.BlockSpec((1, NSA_WIDTH, Q_BLOCK), lambda b, i: (b, 0, i)),
            whole(kc), whole(vct), whole(ks), whole(vst), whole(kw_pad), whole(vwt_pad),
            pl.BlockSpec((1, LANE, Q_BLOCK), lambda b, i: (b, 0, i)),
            pl.BlockSpec((Q_BLOCK, gw), lambda b, i: (b * nq + i, 0)),
            pl.BlockSpec((Q_BLOCK, gw), lambda b, i: (b * nq + i, 1)),
            pl.BlockSpec((SEL_TILE, LANE), lambda b, i: (0, 0)),
        ],
        out_specs=pl.BlockSpec((Q_BLOCK, NSA_WIDTH), lambda b, i: (b * nq + i, 0)),
        scratch_shapes=[
            pltpu.VMEM((N_KV * Q_BLOCK // LANE, SUB + n_cmp_pad, LANE), jnp.float32),
            pltpu.VMEM((S // SEL_LEN + LANE, N_KV * Q_BLOCK), jnp.float32),
            pltpu.VMEM((SEL_TILE, N_COL), jnp.float32),
            pltpu.VMEM((SEL_TILE, N_COL), jnp.float32),
        ],
        compiler_params=_cparams(("arbitrary", "arbitrary")),
        name="nsa",
    )(qt, kc, vct, ks, vst, kw_pad, vwt_pad, gt, z, z, e_mat)


def _out_proj_body(x_ref, ya_ref, yb_ref, wa_ref, wb_ref, o_ref):
    o_ref[...] = (x_ref[...]
                  + jnp.dot(ya_ref[...], wa_ref[...], preferred_element_type=jnp.float32)
                  + jnp.dot(yb_ref[...], wb_ref[...], preferred_element_type=jnp.float32))


def _out_proj(x2, ya, yb, wa, wb):
    rows = x2.shape[0]
    return pl.pallas_call(
        _out_proj_body,
        out_shape=jax.ShapeDtypeStruct((rows, D_MODEL), jnp.float32),
        grid=(rows // OUT_TILE,),
        in_specs=[
            pl.BlockSpec((OUT_TILE, D_MODEL), lambda i: (i, 0)),
            pl.BlockSpec((OUT_TILE, RG_WIDTH), lambda i: (i, 0)),
            pl.BlockSpec((OUT_TILE, NSA_WIDTH), lambda i: (i, 0)),
            pl.BlockSpec((RG_WIDTH, D_MODEL), lambda i: (0, 0)),
            pl.BlockSpec((NSA_WIDTH, D_MODEL), lambda i: (0, 0)),
        ],
        out_specs=pl.BlockSpec((OUT_TILE, D_MODEL), lambda i: (i, 0)),
        compiler_params=_cparams(("arbitrary",)),
        name="out_proj",
    )(x2, ya, yb, wa, wb)


def _rope_tables(pos):
    half = ROPE_DIM // 2
    inv = ROPE_THETA ** (-jnp.arange(half, dtype=jnp.float32) / half)
    ang = pos.astype(jnp.float32)[:, None] * inv[None, :]
    cos, sin = jnp.cos(ang), jnp.sin(ang)
    n = pos.shape[0]
    rest = HEAD_DIM - ROPE_DIM
    ra = jnp.concatenate([cos, cos, jnp.ones((n, rest), jnp.float32)], axis=1)
    rm = jnp.concatenate([-sin, jnp.zeros((n, half + rest), jnp.float32)], axis=1)
    rp = jnp.concatenate([jnp.zeros((n, half), jnp.float32), sin, jnp.zeros((n, rest), jnp.float32)], axis=1)
    return tuple(jnp.tile(t, (1, LANE // HEAD_DIM)) for t in (ra, rm, rp))


def _block_diag(blocks):
    n, r, c = blocks.shape
    eye = jnp.eye(n, dtype=blocks.dtype)
    return (eye[:, None, :, None] * blocks[:, :, None, :]).reshape(n * r, n * c)


def _compress_params(pe, w1, w2):
    bf = jnp.bfloat16
    w1t = w1.reshape(2, N_CHUNK_TOK, HEAD_DIM, CMP_HIDDEN)
    dup = lambda w: jax.vmap(lambda m: _block_diag(jnp.stack([m] * N_KV)))(w)
    wa, wb = dup(w1t[0]).astype(bf), dup(w1t[1]).astype(bf)
    pet = jnp.tile(pe.reshape(2, N_CHUNK_TOK, 1, HEAD_DIM), (1, 1, 1, N_KV))
    w2d = _block_diag(jnp.stack([w2] * N_KV)).astype(bf)
    return wa, wb, pet[0], pet[1], w2d


def _layer(x2, B, S, norm_g, w_in, conv_w, conv_b, rg_wr, rg_br, rg_wi, rg_bi, rg_lam,
           q_g, k_g, pe_k, w1_k, w2_k, pe_v, w1_v, w2_v, w_out, tabs_tok, tabs_cmp, ind):
    bf = jnp.bfloat16
    w_in_p = jnp.pad(w_in, ((0, 0), (0, N_PAD - N_IN))).astype(bf)
    tile2 = lambda v: jnp.tile(v, LANE // HEAD_DIM)[None, :]
    y_a, qt, ks, vst, kw, vwt, gt, kcvc, ng = _front(
        x2, B, S, norm_g[None, :], w_in_p,
        conv_w, conv_b[None, :], _block_diag(rg_wr).astype(bf), rg_br[None, :],
        _block_diag(rg_wi).astype(bf), rg_bi[None, :], rg_lam[None, :],
        *tabs_tok, tile2(q_g), tile2(k_g[1]), tile2(k_g[2]), ind)
    kc, vct = _compress(kcvc, B, S, _compress_params(pe_k, w1_k, w2_k), _compress_params(pe_v, w1_v, w2_v),
                        tile2(k_g[0]), ind, *tabs_cmp)
    y_b = _nsa(ng, qt, kc, vct, ks, vst, kw, vwt, gt, B, S)

    w_out16 = w_out.astype(bf)
    return _out_proj(x2, y_a, y_b, w_out16[:RG_WIDTH], w_out16[RG_WIDTH:])


def kernel(x, norm_g, w_in, conv_w, conv_b, rg_wr, rg_br, rg_wi, rg_bi, rg_lambda, q_norm_g, k_norm_g,
           cmp_pe_k, cmp_w1_k, cmp_w2_k, cmp_pe_v, cmp_w1_v, cmp_w2_v, w_out):
    B, S, D = x.shape
    assert D == D_MODEL and S % ROW_TILE == 0 and S % SEL_TILE == 0 and S >= WINDOW and (B * S) % OUT_TILE == 0
    depth = norm_g.shape[0]
    tabs_tok = _rope_tables(jnp.arange(S))
    tabs_cmp = _rope_tables(jnp.arange(S // CMP_STRIDE) * CMP_STRIDE + (CMP_LEN - 1))
    ind = _block_diag(jnp.ones((LANE // HEAD_DIM, HEAD_DIM, HEAD_DIM), jnp.bfloat16))
    x2 = x.reshape(B * S, D)
    for l in range(depth):
        x2 = _layer(x2, B, S, norm_g[l], w_in[l], conv_w[l], conv_b[l], rg_wr[l], rg_br[l], rg_wi[l],
                    rg_bi[l], rg_lambda[l], q_norm_g[l], k_norm_g[l], cmp_pe_k[l], cmp_w1_k[l],
                    cmp_w2_k[l], cmp_pe_v[l], cmp_w1_v[l], cmp_w2_v[l], w_out[l], tabs_tok, tabs_cmp, ind)
    return x2.reshape(B, S, D)
```

```python
import functools
import math

import jax
import jax.numpy as jnp
from jax import lax
from jax.experimental import pallas as pl
from jax.experimental.pallas import tpu as pltpu

D_MODEL = 1024
RG_WIDTH = 512
RG_BLOCKS = 8
RG_BLOCK = 64
CONV_W = 4
RG_C = 8.0
NSA_WIDTH = 512
HEAD_DIM = 64
N_KV = 2
HPG = 4
KV_WIDTH = N_KV * HEAD_DIM
ROPE_DIM = 16
ROPE_THETA = 500000.0
CMP_LEN = 32
CMP_STRIDE = 16
CMP_HIDDEN = 128
SEL_LEN = 64
SEL_TOPN = 16
WINDOW = 512
N_BRANCH = 3
Q_BLOCK = 256
EPS = 1e-6
NEG = -1e30
N_IN = 2 * RG_WIDTH + 2 * NSA_WIDTH + 6 * KV_WIDTH + N_BRANCH * 8
LANE = 128
MXU_COLS = 256
N_PAD = ((N_IN + LANE - 1) // LANE) * LANE

COL_Q = 2 * RG_WIDTH
COL_KC = COL_Q + NSA_WIDTH
COL_VC = COL_KC + KV_WIDTH
COL_KS = COL_VC + KV_WIDTH
COL_VS = COL_KS + KV_WIDTH
COL_KW = COL_VS + KV_WIDTH
COL_VW = COL_KW + KV_WIDTH
COL_NG = COL_VW + KV_WIDTH
COL_BG = COL_NG + NSA_WIDTH

VMEM_LIMIT = 56 * 1024 * 1024

ROW_TILE = 512
OUT_TILE = 2048
SEL_TILE = 512
SEL_BLOCKS = SEL_TILE // SEL_LEN
LOG2E = 1.4426950408889634


def _cparams(sem):
    return pltpu.CompilerParams(dimension_semantics=sem, vmem_limit_bytes=VMEM_LIMIT)


def _head_mean_sq(x, ind):
    sq = x * x
    hi = sq.astype(jnp.bfloat16)
    lo = (sq - hi.astype(jnp.float32)).astype(jnp.bfloat16)
    tot = (jnp.dot(hi, ind, preferred_element_type=jnp.float32)
           + jnp.dot(lo, ind, preferred_element_type=jnp.float32))
    return tot * (1.0 / HEAD_DIM)


def _norm_rope(x, gain, ind, ra, rm, rp):
    width = x.shape[-1]
    y = (x * lax.rsqrt(_head_mean_sq(x, ind) + EPS)) * gain
    up = pltpu.roll(y, width - ROPE_DIM // 2, axis=1)
    dn = pltpu.roll(y, ROPE_DIM // 2, axis=1)
    return y * ra + up * rm + dn * rp


def _prep_q(zq, first, tabs, qg_ref, ind_ref, qt_ref):
    scale = LOG2E / math.sqrt(HEAD_DIM)
    for c in range(zq.shape[1] // LANE):
        qn = _norm_rope(zq[:, c * LANE:(c + 1) * LANE], qg_ref[...], ind_ref[...], *tabs) * scale
        qt_ref[0, (first + c) * LANE:(first + c + 1) * LANE, :] = qn.T.astype(jnp.bfloat16)


def _prep_kv(zkv, tabs, kg_ref, ind_ref, k_ref, vt_ref):
    k_ref[0] = _norm_rope(zkv[:, :KV_WIDTH], kg_ref[...], ind_ref[...], *tabs).astype(jnp.bfloat16)
    vt_ref[0] = zkv[:, KV_WIDTH:].T.astype(jnp.bfloat16)


RG_TILE = 512
RG_CHUNKS = 16
SUB = 8


def _block_diag_dot(x, w_ref):
    n = w_ref.shape[0] // MXU_COLS
    return jnp.concatenate(
        [jnp.dot(x[:, c * MXU_COLS:(c + 1) * MXU_COLS],
                 w_ref[c * MXU_COLS:(c + 1) * MXU_COLS, c * MXU_COLS:(c + 1) * MXU_COLS],
                 preferred_element_type=jnp.float32) for c in range(n)], axis=1)


def _rglru_tile(x, gate, cw_ref, cb_ref, wr_ref, br_ref, wi_ref, bi_ref, lam_ref,
                y_ref, tail_ref, carry_ref, a_ref, b_ref):
    tail = tail_ref[...]
    row8 = lax.broadcasted_iota(jnp.int32, (SUB, RG_WIDTH), 0)
    conv = x * cw_ref[CONV_W - 1:CONV_W, :] + cb_ref[...]
    for s in range(1, CONV_W):
        xs = pltpu.roll(x, s, axis=0)
        first = jnp.where(row8 < s, pltpu.roll(tail, s, axis=0), xs[0:SUB])
        xs = jnp.concatenate([first, xs[SUB:]], axis=0)
        conv = conv + xs * cw_ref[CONV_W - 1 - s:CONV_W - s, :]
    tail_ref[...] = x[RG_TILE - SUB:]
    yield

    softplus = jax.nn.softplus(-lam_ref[...])
    chunk = RG_TILE // RG_CHUNKS
    grp = (chunk // SUB, SUB, RG_WIDTH)
    row = lax.broadcasted_iota(jnp.int32, grp, 1)
    for c in range(RG_CHUNKS):
        rows = slice(c * chunk, (c + 1) * chunk)
        cv = conv[rows]
        cb16 = cv.astype(jnp.bfloat16)
        r = jax.nn.sigmoid(_block_diag_dot(cb16, wr_ref) + br_ref[...])
        ig = jax.nn.sigmoid(_block_diag_dot(cb16, wi_ref) + bi_ref[...])
        log_a = (-RG_C * r) * softplus
        a = jnp.exp(log_a)
        u = jnp.sqrt(-jnp.tanh(log_a) * (a * a + 1.0)) * (ig * cv)
        a, u = a.reshape(grp), u.reshape(grp)
        for d in (1, 2, 4):
            keep = row >= d
            a_s = jnp.where(keep, pltpu.roll(a, d, axis=1), 1.0)
            u_s = jnp.where(keep, pltpu.roll(u, d, axis=1), 0.0)
            u = a * u_s + u
            a = a * a_s
        a_ref[rows, :] = a.reshape(chunk, RG_WIDTH)
        b_ref[rows, :] = u.reshape(chunk, RG_WIDTH)
        yield

    h = carry_ref[...]
    for c in range(RG_CHUNKS):
        for k in range(c * chunk // SUB, (c + 1) * chunk // SUB):
            grows = slice(k * SUB, (k + 1) * SUB)
            hk = a_ref[grows, :] * h + b_ref[grows, :]
            b_ref[grows, :] = hk
            h = jnp.broadcast_to(hk[SUB - 1:SUB, :], (SUB, RG_WIDTH))
        yield
    carry_ref[...] = h

    for c in range(RG_CHUNKS):
        rows = slice(c * chunk, (c + 1) * chunk)
        gt = gate[rows]
        y_ref[rows, :] = (b_ref[rows, :] * (gt * jax.nn.sigmoid(gt))).astype(y_ref.dtype)
        yield


def _front_body(x_ref, g_ref, w_ref,
                cw_ref, cb_ref, wr_ref, br_ref, wi_ref, bi_ref, lam_ref,
                ra_ref, rm_ref, rp_ref, qg_ref, kgs_ref, kgw_ref, ind_ref,
                ya_ref, qt_ref, kso_ref, vst_ref, kwo_ref, vwt_ref, gt_ref, kcvc_ref, ng_ref,
                tail_ref, carry_ref, a_ref, b_ref):
    @pl.when(pl.program_id(1) == 0)
    def _():
        tail_ref[...] = jnp.zeros_like(tail_ref)
        carry_ref[...] = jnp.zeros_like(carry_ref)

    x = x_ref[...]
    ms = jnp.mean(x * x, axis=-1, keepdims=True)
    h = ((x * lax.rsqrt(ms + EPS)) * g_ref[...]).astype(jnp.bfloat16)
    proj = lambda c0, c1: jnp.dot(h, w_ref[:, c0:c1], preferred_element_type=jnp.float32)

    z_rg = proj(0, COL_Q)
    mixer = _rglru_tile(z_rg[:, :RG_WIDTH], z_rg[:, RG_WIDTH:], cw_ref, cb_ref, wr_ref, br_ref, wi_ref,
                        bi_ref, lam_ref, ya_ref, tail_ref, carry_ref, a_ref, b_ref)
    tabs = (ra_ref[...], rm_ref[...], rp_ref[...])
    half = 2 * KV_WIDTH
    z = {}

    def project(name, c0, c1):
        z[name] = proj(c0, c1)

    def store(ref, c0, c1):
        ref[...] = proj(c0, c1)

    def gates():
        gt_ref[0] = jax.nn.sigmoid(proj(COL_BG, N_PAD)).T

    items = [
        functools.partial(project, "q0", COL_Q, COL_Q + half),
        functools.partial(project, "q1", COL_Q + half, COL_KC),
        functools.partial(project, "sel", COL_KS, COL_KW),
        functools.partial(project, "win", COL_KW, COL_NG),
        functools.partial(store, kcvc_ref, COL_KC, COL_KS),
        lambda: _prep_q(z["q0"], 0, tabs, qg_ref, ind_ref, qt_ref),
        functools.partial(store, ng_ref.at[:, 0:half], COL_NG, COL_NG + half),
        lambda: _prep_q(z["q1"], half // LANE, tabs, qg_ref, ind_ref, qt_ref),
        functools.partial(store, ng_ref.at[:, half:], COL_NG + half, COL_BG),
        lambda: _prep_kv(z["sel"], tabs, kgs_ref, ind_ref, kso_ref, vst_ref),
        gates,
        lambda: _prep_kv(z["win"], tabs, kgw_ref, ind_ref, kwo_ref, vwt_ref),
    ]
    every = (1 + 3 * RG_CHUNKS) // len(items)
    for i, _ in enumerate(mixer):
        if items and i % every == 0:
            items.pop(0)()
    while items:
        items.pop(0)()


def _front(x2, B, S, g, w, cw, cb, wr, br, wi, bi, lam, ra, rm, rp, qg, kgs, kgw, ind):
    nt = S // ROW_TILE
    bf = jnp.bfloat16
    rows = lambda width: pl.BlockSpec((ROW_TILE, width), lambda b, i: (b * nt + i, 0))
    const = lambda a: pl.BlockSpec(a.shape, lambda b, i: (0, 0))
    tab = pl.BlockSpec((ROW_TILE, LANE), lambda b, i: (i, 0))
    tok = lambda width: pl.BlockSpec((1, ROW_TILE, width), lambda b, i: (b, i, 0))
    tr = lambda height: pl.BlockSpec((1, height, ROW_TILE), lambda b, i: (b, 0, i))
    return pl.pallas_call(
        _front_body,
        out_shape=(
            jax.ShapeDtypeStruct((B * S, RG_WIDTH), bf),
            jax.ShapeDtypeStruct((B, NSA_WIDTH, S), bf),
            jax.ShapeDtypeStruct((B, S, KV_WIDTH), bf),
            jax.ShapeDtypeStruct((B, KV_WIDTH, S), bf),
            jax.ShapeDtypeStruct((B, S, KV_WIDTH), bf),
            jax.ShapeDtypeStruct((B, KV_WIDTH, S), bf),
            jax.ShapeDtypeStruct((B, LANE, S), jnp.float32),
            jax.ShapeDtypeStruct((B * S, 2 * KV_WIDTH), jnp.float32),
            jax.ShapeDtypeStruct((B * S, NSA_WIDTH), jnp.float32),
        ),
        grid=(B, nt),
        in_specs=[rows(D_MODEL), const(g), const(w),
                  const(cw), const(cb), const(wr), const(br), const(wi), const(bi), const(lam),
                  tab, tab, tab, const(qg), const(kgs), const(kgw), const(ind)],
        out_specs=(rows(RG_WIDTH), tr(NSA_WIDTH), tok(KV_WIDTH), tr(KV_WIDTH), tok(KV_WIDTH), tr(KV_WIDTH),
                   tr(LANE), rows(2 * KV_WIDTH), rows(NSA_WIDTH)),
        scratch_shapes=[
            pltpu.VMEM((SUB, RG_WIDTH), jnp.float32),
            pltpu.VMEM((SUB, RG_WIDTH), jnp.float32),
            pltpu.VMEM((RG_TILE, RG_WIDTH), jnp.float32),
            pltpu.VMEM((RG_TILE, RG_WIDTH), jnp.float32),
        ],
        compiler_params=_cparams(("arbitrary", "arbitrary")),
        name="front",
    )(x2, g, w, cw, cb, wr, br, wi, bi, lam, ra, rm, rp, qg, kgs, kgw, ind)


N_CHUNK_TOK = CMP_STRIDE


def _compress_one(src_ref, wa_ref, wb_ref, pea_ref, peb_ref, w2_ref, n_chunk):
    ha = jnp.zeros((n_chunk, 2 * CMP_HIDDEN), jnp.float32)
    hb = jnp.zeros((n_chunk, 2 * CMP_HIDDEN), jnp.float32)
    for j in range(N_CHUNK_TOK):
        xj = src_ref[pl.ds(j, n_chunk, stride=N_CHUNK_TOK), :]
        ha = ha + jnp.dot((xj + pea_ref[j]).astype(jnp.bfloat16), wa_ref[j],
                          preferred_element_type=jnp.float32)
        hb = hb + jnp.dot((xj + peb_ref[j]).astype(jnp.bfloat16), wb_ref[j],
                          preferred_element_type=jnp.float32)
    hid = ha + pltpu.roll(hb, n_chunk - 1, axis=0)
    act = hid * jax.nn.sigmoid(hid)
    return jnp.dot(act.astype(jnp.bfloat16), w2_ref[...], preferred_element_type=jnp.float32)


def _compress_body(kc_ref, vc_ref, wak_ref, wbk_ref, peak_ref, pebk_ref, w2k_ref,
                   wav_ref, wbv_ref, peav_ref, pebv_ref, w2v_ref,
                   kg_ref, ind_ref, ra_ref, rm_ref, rp_ref, ko_ref, vto_ref, *, n_chunk):
    kc = _compress_one(kc_ref, wak_ref, wbk_ref, peak_ref, pebk_ref, w2k_ref, n_chunk)
    ko_ref[0] = _norm_rope(kc, kg_ref[...], ind_ref[...], ra_ref[...], rm_ref[...],
                           rp_ref[...]).astype(jnp.bfloat16)
    vc = _compress_one(vc_ref, wav_ref, wbv_ref, peav_ref, pebv_ref, w2v_ref, n_chunk)
    vto_ref[0] = vc.T.astype(jnp.bfloat16)


def _compress(z, B, S, wk, wv, kg, ind, ra, rm, rp):
    n_chunk = S // N_CHUNK_TOK

    def full(a):
        nd = a.ndim
        return pl.BlockSpec(a.shape, lambda b: (0,) * nd)

    consts = list(wk) + list(wv) + [kg, ind, ra, rm, rp]
    return pl.pallas_call(
        functools.partial(_compress_body, n_chunk=n_chunk),
        out_shape=(
            jax.ShapeDtypeStruct((B, n_chunk, KV_WIDTH), jnp.bfloat16),
            jax.ShapeDtypeStruct((B, KV_WIDTH, n_chunk), jnp.bfloat16),
        ),
        grid=(B,),
        in_specs=[
            pl.BlockSpec((S, KV_WIDTH), lambda b: (b, 0)),
            pl.BlockSpec((S, KV_WIDTH), lambda b: (b, 1)),
        ] + [full(a) for a in consts],
        out_specs=(
            pl.BlockSpec((1, n_chunk, KV_WIDTH), lambda b: (b, 0, 0)),
            pl.BlockSpec((1, KV_WIDTH, n_chunk), lambda b: (b, 0, 0)),
        ),
        compiler_params=_cparams(("arbitrary",)),
        name="compress",
    )(z, z, *consts)


G_COL = HPG * Q_BLOCK
N_COL = N_KV * G_COL


def _group_dots(vts, p):
    p16 = p.astype(jnp.bfloat16)
    return jnp.concatenate(
        [jnp.dot(vt, p16[:, g * G_COL:(g + 1) * G_COL], preferred_element_type=jnp.float32)
         for g, vt in enumerate(vts)], axis=1)


N_VARIANT = 8
MASK_ROWS = 128
ONES_ROWS = 16
ACC_ROWS = HEAD_DIM + ONES_ROWS


def _with_ones(vt):
    return jnp.concatenate([vt, jnp.ones((ONES_ROWS, vt.shape[1]), vt.dtype)], axis=0)


N_CBLK = N_COL // MXU_COLS


def _cblk(c):
    return slice(c * MXU_COLS, (c + 1) * MXU_COLS)


def _tile_softmax(blocks, vts):
    vones = [_with_ones(vt) for vt in vts]
    ms, accs = [], []
    for c in range(N_CBLK):
        sc = blocks(c)
        m = jnp.max(sc, axis=0, keepdims=True)
        p = jnp.exp2(sc - m).astype(jnp.bfloat16)
        ms.append(m)
        accs.append(jnp.dot(vones[c * MXU_COLS // G_COL], p, preferred_element_type=jnp.float32))
    return jnp.concatenate(ms, axis=1), jnp.concatenate(accs, axis=1)


def _merge(state, tile):
    m0, acc0 = state
    m1, acc1 = tile
    m = jnp.maximum(m0, m1)
    return m, jnp.exp2(m0 - m) * acc0 + jnp.exp2(m1 - m) * acc1


def _finish(state):
    _, acc = state
    return acc[0:HEAD_DIM] * (1.0 / acc[HEAD_DIM:HEAD_DIM + 1])


def _nsa_body(qt_ref, kc_ref, vct_ref, ks_ref, vst_ref, kw_ref, vwt_ref, gt_ref, ng0_ref, ng1_ref, e_ref,
              y_ref, ps_ref, bias_ref, sa_ref, sb_ref, *, n_cmp_pad):
    qi = pl.program_id(1)
    q0 = qi * Q_BLOCK
    groups = range(N_KV)
    grows = lambda g: slice(g * HEAD_DIM, (g + 1) * HEAD_DIM)

    qt = qt_ref[0]
    zero = jnp.zeros((HEAD_DIM, G_COL), jnp.bfloat16)
    qrows = []
    for g in groups:
        qcat = jnp.concatenate([qt[(g * HPG + h) * HEAD_DIM:(g * HPG + h + 1) * HEAD_DIM, :]
                                for h in range(HPG)], axis=1)
        qrows.append(jnp.concatenate([qcat if gg == g else zero for gg in groups], axis=1))
    qz = jnp.concatenate(qrows, axis=0)

    tl = lax.broadcasted_iota(jnp.int32, (1, N_COL), 1) % Q_BLOCK
    tcol = q0 + tl

    per = SEL_LEN // CMP_STRIDE
    n_sel_all = n_cmp_pad // per

    def compressed_and_topk(n_cmp):
        sc = jnp.dot(kc_ref[0, 0:n_cmp, :], qz, preferred_element_type=jnp.float32)
        n_open_rows = max(n_cmp - MASK_ROWS, 0)
        crow = n_open_rows + lax.broadcasted_iota(jnp.int32, (n_cmp - n_open_rows, N_COL), 0)
        cvalid = crow <= (tcol - (CMP_LEN - 1)) // CMP_STRIDE
        scm = jnp.concatenate(([sc[0:n_open_rows]] if n_open_rows else [])
                              + [jnp.where(cvalid, sc[n_open_rows:], NEG)], axis=0)
        mc = jnp.max(scm, axis=0, keepdims=True)
        pc = jnp.exp2(scm - mc)
        lc = jnp.sum(pc, axis=0, keepdims=True)
        pc = pc * jnp.where(tcol >= CMP_LEN - 1, 1.0 / lc, 0.0)
        o_c = _group_dots([vct_ref[0, grows(g), 0:n_cmp] for g in groups], pc)

        n_sel = n_cmp // per
        tq = q0 + lax.broadcasted_iota(jnp.int32, (n_sel, Q_BLOCK), 1)
        bt = tq // SEL_LEN
        rowj = lax.broadcasted_iota(jnp.int32, (n_sel, Q_BLOCK), 0)
        forced = (rowj == 0) | (rowj == bt) | (rowj == bt - 1)
        valid = rowj <= bt
        rowf = rowj.astype(jnp.float32)
        cands = []
        for g in groups:
            psum = pc[:, g * G_COL:g * G_COL + Q_BLOCK]
            for h in range(1, HPG):
                psum = psum + pc[:, g * G_COL + h * Q_BLOCK:g * G_COL + (h + 1) * Q_BLOCK]
            nl = Q_BLOCK // LANE
            for l in range(nl):
                ps_ref[g * nl + l, 0:SUB, :] = jnp.zeros((SUB, LANE), jnp.float32)
                ps_ref[g * nl + l, SUB:SUB + n_cmp, :] = psum[:, l * LANE:(l + 1) * LANE]
            strided = lambda k: jnp.concatenate(
                [ps_ref[g * nl + l, pl.ds(SUB + k, n_sel, stride=per), :] for l in range(nl)], axis=1)
            imp = strided(0) + strided(1) + strided(2) + 0.5 * strided(3) + 0.5 * strided(-1)
            cands.append(jnp.where(valid & jnp.logical_not(forced), imp, -1.0))

        eligible = [c >= 0.0 for c in cands]
        for _ in range(SEL_TOPN - 3):
            for g in groups:
                mx = jnp.max(cands[g], axis=0, keepdims=True)
                first = jnp.min(jnp.where(cands[g] == mx, rowf, float(n_sel)), axis=0, keepdims=True)
                cands[g] = jnp.where(rowf == first, -1.0, cands[g])
        taken = [(forced & valid) | (eligible[g] & (cands[g] < 0.0)) for g in groups]
        bias_ref[0:n_sel, :] = jnp.concatenate([jnp.where(tk, 0.0, NEG) for tk in taken], axis=1)
        if n_sel < n_sel_all:
            bias_ref[n_sel:n_sel_all, :] = jnp.full((n_sel_all - n_sel, N_KV * Q_BLOCK), NEG, jnp.float32)
        bias_ref[n_sel_all:, :] = jnp.zeros((LANE, N_KV * Q_BLOCK), jnp.float32)
        return o_c

    n_var = N_VARIANT
    quarter = n_cmp_pad // n_var
    assert MASK_ROWS >= quarter + Q_BLOCK // CMP_STRIDE
    variant = jnp.minimum((q0 + Q_BLOCK - CMP_LEN) // CMP_STRIDE // quarter, n_var - 1)
    o_c = lax.switch(variant, [functools.partial(compressed_and_topk, (v + 1) * quarter) for v in range(n_var)])

    e_mat = e_ref[...]

    def produce(dst_ref, t):
        koff = pl.multiple_of(t * SEL_TILE, SEL_TILE)
        boff = pl.multiple_of(t * SEL_BLOCKS, SEL_BLOCKS)
        brows = bias_ref[pl.ds(boff, LANE), :].astype(jnp.bfloat16)
        bcols = jnp.concatenate([brows[:, g * Q_BLOCK:(g + 1) * Q_BLOCK] for g in groups for _ in range(HPG)],
                                axis=1)
        rhs = jnp.concatenate([qz, bcols], axis=0)
        lhs = jnp.concatenate([ks_ref[0, pl.ds(koff, SEL_TILE), :], e_mat], axis=1)

        def block(c):
            dst_ref[:, _cblk(c)] = jnp.dot(lhs, rhs[:, _cblk(c)], preferred_element_type=jnp.float32)
        return block

    def sel_values(t):
        koff = pl.multiple_of(t * SEL_TILE, SEL_TILE)
        return [vst_ref[0, grows(g), pl.ds(koff, SEL_TILE)] for g in groups]

    def consume(s_ref, t, state, make_next=None, causal=False):
        srow = lax.broadcasted_iota(jnp.int32, (SEL_TILE, MXU_COLS), 0)
        limit = tcol - t * SEL_TILE

        def blocks(c):
            if make_next is not None:
                make_next(c)
            sc = s_ref[:, _cblk(c)]
            return jnp.where(srow <= limit[:, _cblk(c)], sc, NEG) if causal else sc
        return _merge(state, _tile_softmax(blocks, sel_values(t)))

    def pair(j, state):
        t = 2 * j
        state = consume(sa_ref, t, state, produce(sb_ref, t + 1))
        return consume(sb_ref, t + 1, state, produce(sa_ref, t + 2))

    n_open = q0 // SEL_TILE

    win_len = WINDOW + Q_BLOCK
    woff = pl.multiple_of(q0, Q_BLOCK)
    flag = jnp.where(lax.broadcasted_iota(jnp.int32, (LANE, N_COL), 0) == 0, NEG, 0.0)
    first_tile = produce(sa_ref, 0)
    for c in range(N_CBLK):
        first_tile(c)
    sw = jnp.dot(kw_ref[0, pl.ds(woff, win_len), :],
                 jnp.concatenate([qz, flag.astype(jnp.bfloat16)], axis=0),
                 preferred_element_type=jnp.float32)
    wrow = lax.broadcasted_iota(jnp.int32, (Q_BLOCK, N_COL), 0)
    sw = jnp.concatenate([jnp.where(wrow > tl, sw[0:Q_BLOCK], NEG)]
                         + ([sw[Q_BLOCK:WINDOW]] if Q_BLOCK < WINDOW else [])
                         + [jnp.where(wrow <= tl, sw[WINDOW:], NEG)], axis=0)
    o_w = _finish(_tile_softmax(lambda c: sw[:, _cblk(c)],
                                [vwt_ref[0, grows(g), pl.ds(woff, win_len)] for g in groups]))

    state = (jnp.full((1, N_COL), NEG, jnp.float32), jnp.zeros((ACC_ROWS, N_COL), jnp.float32))
    n_tail = jnp.where(n_open == 0, 0, 2 - n_open % 2)
    n_pair = (n_open - n_tail) // 2
    state = lax.fori_loop(0, n_pair // 2, lambda j, st: pair(2 * j + 1, pair(2 * j, st)), state)
    state = lax.fori_loop(2 * (n_pair // 2), n_pair, pair, state)
    t_next = 2 * n_pair

    def tail0(state):
        return _finish(consume(sa_ref, t_next, state, causal=True))

    def tail1(state):
        state = consume(sa_ref, t_next, state, produce(sb_ref, t_next + 1))
        return _finish(consume(sb_ref, t_next + 1, state, causal=True))

    def tail2(state):
        state = consume(sa_ref, t_next, state, produce(sb_ref, t_next + 1))
        state = consume(sb_ref, t_next + 1, state, produce(sa_ref, t_next + 2))
        return _finish(consume(sa_ref, t_next + 2, state, causal=True))

    o_s = lax.switch(n_tail, [tail0, tail1, tail2], state)

    def gate_row(branch):
        return jnp.concatenate(
            [gt_ref[0, r * N_BRANCH + branch:r * N_BRANCH + branch + 1, :] for r in range(N_KV * HPG)], axis=1)

    yt = gate_row(0) * o_c + gate_row(1) * o_s + gate_row(2) * o_w
    halves = []
    for pr in range(N_KV * HPG // 2):
        two = jnp.concatenate([yt[:, (2 * pr) * Q_BLOCK:(2 * pr + 1) * Q_BLOCK],
                               yt[:, (2 * pr + 1) * Q_BLOCK:(2 * pr + 2) * Q_BLOCK]], axis=0)
        halves.append(two.T)
    y = jnp.concatenate(halves, axis=1)
    ng = jnp.concatenate([ng0_ref[...], ng1_ref[...]], axis=1)
    y_ref[...] = (y * (ng * jax.nn.sigmoid(ng))).astype(y_ref.dtype)


def _nsa(z, qt, kc, vct, ks, vst, kw, vwt, gt, B, S):
    nq = S // Q_BLOCK
    n_cmp_pad = S // CMP_STRIDE
    gw = HPG * HEAD_DIM
    bf = jnp.bfloat16
    e_mat = (jnp.arange(SEL_TILE)[:, None] // SEL_LEN == jnp.arange(LANE)[None, :]).astype(bf)
    lead = jnp.zeros((B, WINDOW, 2 * KV_WIDTH), bf).at[:, :, KV_WIDTH].set(1.0)
    kw_pad = jnp.concatenate([lead, jnp.concatenate([kw, jnp.zeros_like(kw)], axis=2)], axis=1)
    vwt_pad = jnp.pad(vwt, ((0, 0), (0, 0), (WINDOW, 0)))
    whole = lambda a: pl.BlockSpec((1,) + a.shape[1:], lambda b, i: (b, 0, 0))
    return pl.pallas_call(
        functools.partial(_nsa_body, n_cmp_pad=n_cmp_pad),
        out_shape=jax.ShapeDtypeStruct((B * S, NSA_WIDTH), bf),
        grid=(B, nq),
        in_specs=[
            pl.BlockSpec((1, NSA_WIDTH, Q_BLOCK), lambda b, i: (b, 0, i)),
            whole(kc), whole(vct), whole(ks), whole(vst), whole(kw_pad), whole(vwt_pad),
            pl.BlockSpec((1, LANE, Q_BLOCK), lambda b, i: (b, 0, i)),
            pl.BlockSpec((Q_BLOCK, gw), lambda b, i: (b * nq + i, 0)),
            pl.BlockSpec((Q_BLOCK, gw), lambda b, i: (b * nq + i, 1)),
            pl.BlockSpec((SEL_TILE, LANE), lambda b, i: (0, 0)),
        ],
        out_specs=pl.BlockSpec((Q_BLOCK, NSA_WIDTH), lambda b, i: (b * nq + i, 0)),
        scratch_shapes=[
            pltpu.VMEM((N_KV * Q_BLOCK // LANE, SUB + n_cmp_pad, LANE), jnp.float32),
            pltpu.VMEM((S // SEL_LEN + LANE, N_KV * Q_BLOCK), jnp.float32),
            pltpu.VMEM((SEL_TILE, N_COL), jnp.float32),
            pltpu.VMEM((SEL_TILE, N_COL), jnp.float32),
        ],
        compiler_params=_cparams(("arbitrary", "arbitrary")),
        name="nsa",
    )(qt, kc, vct, ks, vst, kw_pad, vwt_pad, gt, z, z, e_mat)


def _out_proj_body(x_ref, ya_ref, yb_ref, wa_ref, wb_ref, o_ref):
    o_ref[...] = (x_ref[...]
                  + jnp.dot(ya_ref[...], wa_ref[...], preferred_element_type=jnp.float32)
                  + jnp.dot(yb_ref[...], wb_ref[...], preferred_element_type=jnp.float32))


def _out_proj(x2, ya, yb, wa, wb):
    rows = x2.shape[0]
    return pl.pallas_call(
        _out_proj_body,
        out_shape=jax.ShapeDtypeStruct((rows, D_MODEL), jnp.float32),
        grid=(rows // OUT_TILE,),
        in_specs=[
            pl.BlockSpec((OUT_TILE, D_MODEL), lambda i: (i, 0)),
            pl.BlockSpec((OUT_TILE, RG_WIDTH), lambda i: (i, 0)),
            pl.BlockSpec((OUT_TILE, NSA_WIDTH), lambda i: (i, 0)),
            pl.BlockSpec((RG_WIDTH, D_MODEL), lambda i: (0, 0)),
            pl.BlockSpec((NSA_WIDTH, D_MODEL), lambda i: (0, 0)),
        ],
        out_specs=pl.BlockSpec((OUT_TILE, D_MODEL), lambda i: (i, 0)),
        compiler_params=_cparams(("arbitrary",)),
        name="out_proj",
    )(x2, ya, yb, wa, wb)


def _rope_tables(pos):
    half = ROPE_DIM // 2
    inv = ROPE_THETA ** (-jnp.arange(half, dtype=jnp.float32) / half)
    ang = pos.astype(jnp.float32)[:, None] * inv[None, :]
    cos, sin = jnp.cos(ang), jnp.sin(ang)
    n = pos.shape[0]
    rest = HEAD_DIM - ROPE_DIM
    ra = jnp.concatenate([cos, cos, jnp.ones((n, rest), jnp.float32)], axis=1)
    rm = jnp.concatenate([-sin, jnp.zeros((n, half + rest), jnp.float32)], axis=1)
    rp = jnp.concatenate([jnp.zeros((n, half), jnp.float32), sin, jnp.zeros((n, rest), jnp.float32)], axis=1)
    return tuple(jnp.tile(t, (1, LANE // HEAD_DIM)) for t in (ra, rm, rp))


def _block_diag(blocks):
    n, r, c = blocks.shape
    eye = jnp.eye(n, dtype=blocks.dtype)
    return (eye[:, None, :, None] * blocks[:, :, None, :]).reshape(n * r, n * c)


def _compress_params(pe, w1, w2):
    bf = jnp.bfloat16
    w1t = w1.reshape(2, N_CHUNK_TOK, HEAD_DIM, CMP_HIDDEN)
    dup = lambda w: jax.vmap(lambda m: _block_diag(jnp.stack([m] * N_KV)))(w)
    wa, wb = dup(w1t[0]).astype(bf), dup(w1t[1]).astype(bf)
    pet = jnp.tile(pe.reshape(2, N_CHUNK_TOK, 1, HEAD_DIM), (1, 1, 1, N_KV))
    w2d = _block_diag(jnp.stack([w2] * N_KV)).astype(bf)
    return wa, wb, pet[0], pet[1], w2d


def _layer(x2, B, S, norm_g, w_in, conv_w, conv_b, rg_wr, rg_br, rg_wi, rg_bi, rg_lam,
           q_g, k_g, pe_k, w1_k, w2_k, pe_v, w1_v, w2_v, w_out, tabs_tok, tabs_cmp, ind):
    bf = jnp.bfloat16
    w_in_p = jnp.pad(w_in, ((0, 0), (0, N_PAD - N_IN))).astype(bf)
    tile2 = lambda v: jnp.tile(v, LANE // HEAD_DIM)[None, :]
    y_a, qt, ks, vst, kw, vwt, gt, kcvc, ng = _front(
        x2, B, S, norm_g[None, :], w_in_p,
        conv_w, conv_b[None, :], _block_diag(rg_wr).astype(bf), rg_br[None, :],
        _block_diag(rg_wi).astype(bf), rg_bi[None, :], rg_lam[None, :],
        *tabs_tok, tile2(q_g), tile2(k_g[1]), tile2(k_g[2]), ind)
    kc, vct = _compress(kcvc, B, S, _compress_params(pe_k, w1_k, w2_k), _compress_params(pe_v, w1_v, w2_v),
                        tile2(k_g[0]), ind, *tabs_cmp)
    y_b = _nsa(ng, qt, kc, vct, ks, vst, kw, vwt, gt, B, S)

    w_out16 = w_out.astype(bf)
    return _out_proj(x2, y_a, y_b, w_out16[:RG_WIDTH], w_out16[RG_WIDTH:])


def kernel(x, norm_g, w_in, conv_w, conv_b, rg_wr, rg_br, rg_wi, rg_bi, rg_lambda, q_norm_g, k_norm_g,
           cmp_pe_k, cmp_w1_k, cmp_w2_k, cmp_pe_v, cmp_w1_v, cmp_w2_v, w_out):
    B, S, D = x.shape
    assert D == D_MODEL and S % ROW_TILE == 0 and S % SEL_TILE == 0 and S >= WINDOW and (B * S) % OUT_TILE == 0
    depth = norm_g.shape[0]
    tabs_tok = _rope_tables(jnp.arange(S))
    tabs_cmp = _rope_tables(jnp.arange(S // CMP_STRIDE) * CMP_STRIDE + (CMP_LEN - 1))
    ind = _block_diag(jnp.ones((LANE // HEAD_DIM, HEAD_DIM, HEAD_DIM), jnp.bfloat16))
    x2 = x.reshape(B * S, D)
    for l in range(depth):
        x2 = _layer(x2, B, S, norm_g[l], w_in[l], conv_w[l], conv_b[l], rg_wr[l], rg_br[l], rg_wi[l],
                    rg_bi[l], rg_lambda[l], q_norm_g[l], k_norm_g[l], cmp_pe_k[l], cmp_w1_k[l],
                    cmp_w2_k[l], cmp_pe_v[l], cmp_w1_v[l], cmp_w2_v[l], w_out[l], tabs_tok, tabs_cmp, ind)
    return x2.reshape(B, S, D)
```

```python
import functools
import math

import jax
import jax.numpy as jnp
from jax import lax
from jax.experimental import pallas as pl
from jax.experimental.pallas import tpu as pltpu

D_MODEL = 1024
RG_WIDTH = 512
RG_BLOCKS = 8
RG_BLOCK = 64
CONV_W = 4
RG_C = 8.0
NSA_WIDTH = 512
HEAD_DIM = 64
N_KV = 2
HPG = 4
KV_WIDTH = N_KV * HEAD_DIM
ROPE_DIM = 16
ROPE_THETA = 500000.0
CMP_LEN = 32
CMP_STRIDE = 16
CMP_HIDDEN = 128
SEL_LEN = 64
SEL_TOPN = 16
WINDOW = 512
N_BRANCH = 3
Q_BLOCK = 256
EPS = 1e-6
NEG = -1e30
N_IN = 2 * RG_WIDTH + 2 * NSA_WIDTH + 6 * KV_WIDTH + N_BRANCH * 8
LANE = 128
MXU_COLS = 256
N_PAD = ((N_IN + LANE - 1) // LANE) * LANE

COL_Q = 2 * RG_WIDTH
COL_KC = COL_Q + NSA_WIDTH
COL_VC = COL_KC + KV_WIDTH
COL_KS = COL_VC + KV_WIDTH
COL_VS = COL_KS + KV_WIDTH
COL_KW = COL_VS + KV_WIDTH
COL_VW = COL_KW + KV_WIDTH
COL_NG = COL_VW + KV_WIDTH
COL_BG = COL_NG + NSA_WIDTH

VMEM_LIMIT = 56 * 1024 * 1024

ROW_TILE = 512
OUT_TILE = 2048
SEL_TILE = 512
SEL_BLOCKS = SEL_TILE // SEL_LEN
LOG2E = 1.4426950408889634


def _cparams(sem):
    return pltpu.CompilerParams(dimension_semantics=sem, vmem_limit_bytes=VMEM_LIMIT)


def _head_mean_sq(x, ind):
    sq = x * x
    hi = sq.astype(jnp.bfloat16)
    lo = (sq - hi.astype(jnp.float32)).astype(jnp.bfloat16)
    tot = (jnp.dot(hi, ind, preferred_element_type=jnp.float32)
           + jnp.dot(lo, ind, preferred_element_type=jnp.float32))
    return tot * (1.0 / HEAD_DIM)


def _norm_rope(x, gain, ind, ra, rm, rp):
    width = x.shape[-1]
    y = (x * lax.rsqrt(_head_mean_sq(x, ind) + EPS)) * gain
    up = pltpu.roll(y, width - ROPE_DIM // 2, axis=1)
    dn = pltpu.roll(y, ROPE_DIM // 2, axis=1)
    return y * ra + up * rm + dn * rp


def _prep_q(zq, first, tabs, qg_ref, ind_ref, qt_ref):
    scale = LOG2E / math.sqrt(HEAD_DIM)
    for c in range(zq.shape[1] // LANE):
        qn = _norm_rope(zq[:, c * LANE:(c + 1) * LANE], qg_ref[...], ind_ref[...], *tabs) * scale
        qt_ref[0, (first + c) * LANE:(first + c + 1) * LANE, :] = qn.T.astype(jnp.bfloat16)


def _prep_kv(zkv, tabs, kg_ref, ind_ref, k_ref, vt_ref):
    k_ref[0] = _norm_rope(zkv[:, :KV_WIDTH], kg_ref[...], ind_ref[...], *tabs).astype(jnp.bfloat16)
    vt_ref[0] = zkv[:, KV_WIDTH:].T.astype(jnp.bfloat16)


RG_TILE = 512
RG_CHUNKS = 16
SUB = 8


def _block_diag_dot(x, w_ref):
    n = w_ref.shape[0] // MXU_COLS
    return jnp.concatenate(
        [jnp.dot(x[:, c * MXU_COLS:(c + 1) * MXU_COLS],
                 w_ref[c * MXU_COLS:(c + 1) * MXU_COLS, c * MXU_COLS:(c + 1) * MXU_COLS],
                 preferred_element_type=jnp.float32) for c in range(n)], axis=1)


def _rglru_tile(x, gate, cw_ref, cb_ref, wr_ref, br_ref, wi_ref, bi_ref, lam_ref,
                y_ref, tail_ref, carry_ref, a_ref, b_ref):
    tail = tail_ref[...]
    row8 = lax.broadcasted_iota(jnp.int32, (SUB, RG_WIDTH), 0)
    conv = x * cw_ref[CONV_W - 1:CONV_W, :] + cb_ref[...]
    for s in range(1, CONV_W):
        xs = pltpu.roll(x, s, axis=0)
        first = jnp.where(row8 < s, pltpu.roll(tail, s, axis=0), xs[0:SUB])
        xs = jnp.concatenate([first, xs[SUB:]], axis=0)
        conv = conv + xs * cw_ref[CONV_W - 1 - s:CONV_W - s, :]
    tail_ref[...] = x[RG_TILE - SUB:]
    yield

    softplus = jax.nn.softplus(-lam_ref[...])
    chunk = RG_TILE // RG_CHUNKS
    grp = (chunk // SUB, SUB, RG_WIDTH)
    row = lax.broadcasted_iota(jnp.int32, grp, 1)
    for c in range(RG_CHUNKS):
        rows = slice(c * chunk, (c + 1) * chunk)
        cv = conv[rows]
        cb16 = cv.astype(jnp.bfloat16)
        r = jax.nn.sigmoid(_block_diag_dot(cb16, wr_ref) + br_ref[...])
        ig = jax.nn.sigmoid(_block_diag_dot(cb16, wi_ref) + bi_ref[...])
        log_a = (-RG_C * r) * softplus
        a = jnp.exp(log_a)
        u = jnp.sqrt(-jnp.tanh(log_a) * (a * a + 1.0)) * (ig * cv)
        a, u = a.reshape(grp), u.reshape(grp)
        for d in (1, 2, 4):
            keep = row >= d
            a_s = jnp.where(keep, pltpu.roll(a, d, axis=1), 1.0)
            u_s = jnp.where(keep, pltpu.roll(u, d, axis=1), 0.0)
            u = a * u_s + u
            a = a * a_s
        a_ref[rows, :] = a.reshape(chunk, RG_WIDTH)
        b_ref[rows, :] = u.reshape(chunk, RG_WIDTH)
        yield

    h = carry_ref[...]
    for c in range(RG_CHUNKS):
        for k in range(c * chunk // SUB, (c + 1) * chunk // SUB):
            grows = slice(k * SUB, (k + 1) * SUB)
            hk = a_ref[grows, :] * h + b_ref[grows, :]
            b_ref[grows, :] = hk
            h = jnp.broadcast_to(hk[SUB - 1:SUB, :], (SUB, RG_WIDTH))
        yield
    carry_ref[...] = h

    for c in range(RG_CHUNKS):
        rows = slice(c * chunk, (c + 1) * chunk)
        gt = gate[rows]
        y_ref[rows, :] = (b_ref[rows, :] * (gt * jax.nn.sigmoid(gt))).astype(y_ref.dtype)
        yield


def _front_body(x_ref, g_ref, w_ref,
                cw_ref, cb_ref, wr_ref, br_ref, wi_ref, bi_ref, lam_ref,
                ra_ref, rm_ref, rp_ref, qg_ref, kgs_ref, kgw_ref, ind_ref,
                ya_ref, qt_ref, kso_ref, vst_ref, kwo_ref, vwt_ref, gt_ref, kcvc_ref, ng_ref,
                tail_ref, carry_ref, a_ref, b_ref):
    @pl.when(pl.program_id(1) == 0)
    def _():
        tail_ref[...] = jnp.zeros_like(tail_ref)
        carry_ref[...] = jnp.zeros_like(carry_ref)

    x = x_ref[...]
    ms = jnp.mean(x * x, axis=-1, keepdims=True)
    h = ((x * lax.rsqrt(ms + EPS)) * g_ref[...]).astype(jnp.bfloat16)
    proj = lambda c0, c1: jnp.dot(h, w_ref[:, c0:c1], preferred_element_type=jnp.float32)

    z_rg = proj(0, COL_Q)
    mixer = _rglru_tile(z_rg[:, :RG_WIDTH], z_rg[:, RG_WIDTH:], cw_ref, cb_ref, wr_ref, br_ref, wi_ref,
                        bi_ref, lam_ref, ya_ref, tail_ref, carry_ref, a_ref, b_ref)
    tabs = (ra_ref[...], rm_ref[...], rp_ref[...])
    half = 2 * KV_WIDTH
    z = {}

    def project(name, c0, c1):
        z[name] = proj(c0, c1)

    def store(ref, c0, c1):
        ref[...] = proj(c0, c1)

    def gates():
        gt_ref[0] = jax.nn.sigmoid(proj(COL_BG, N_PAD)).T

    items = [
        functools.partial(project, "q0", COL_Q, COL_Q + half),
        functools.partial(project, "q1", COL_Q + half, COL_KC),
        functools.partial(project, "sel", COL_KS, COL_KW),
        functools.partial(project, "win", COL_KW, COL_NG),
        functools.partial(store, kcvc_ref, COL_KC, COL_KS),
        lambda: _prep_q(z["q0"], 0, tabs, qg_ref, ind_ref, qt_ref),
        functools.partial(store, ng_ref.at[:, 0:half], COL_NG, COL_NG + half),
        lambda: _prep_q(z["q1"], half // LANE, tabs, qg_ref, ind_ref, qt_ref),
        functools.partial(store, ng_ref.at[:, half:], COL_NG + half, COL_BG),
        lambda: _prep_kv(z["sel"], tabs, kgs_ref, ind_ref, kso_ref, vst_ref),
        gates,
        lambda: _prep_kv(z["win"], tabs, kgw_ref, ind_ref, kwo_ref, vwt_ref),
    ]
    every = (1 + 3 * RG_CHUNKS) // len(items)
    for i, _ in enumerate(mixer):
        if items and i % every == 0:
            items.pop(0)()
    while items:
        items.pop(0)()


def _front(x2, B, S, g, w, cw, cb, wr, br, wi, bi, lam, ra, rm, rp, qg, kgs, kgw, ind):
    nt = S // ROW_TILE
    bf = jnp.bfloat16
    rows = lambda width: pl.BlockSpec((ROW_TILE, width), lambda b, i: (b * nt + i, 0))
    const = lambda a: pl.BlockSpec(a.shape, lambda b, i: (0, 0))
    tab = pl.BlockSpec((ROW_TILE, LANE), lambda b, i: (i, 0))
    tok = lambda width: pl.BlockSpec((1, ROW_TILE, width), lambda b, i: (b, i, 0))
    tr = lambda height: pl.BlockSpec((1, height, ROW_TILE), lambda b, i: (b, 0, i))
    return pl.pallas_call(
        _front_body,
        out_shape=(
            jax.ShapeDtypeStruct((B * S, RG_WIDTH), bf),
            jax.ShapeDtypeStruct((B, NSA_WIDTH, S), bf),
            jax.ShapeDtypeStruct((B, S, KV_WIDTH), bf),
            jax.ShapeDtypeStruct((B, KV_WIDTH, S), bf),
            jax.ShapeDtypeStruct((B, S, KV_WIDTH), bf),
            jax.ShapeDtypeStruct((B, KV_WIDTH, S), bf),
            jax.ShapeDtypeStruct((B, LANE, S), jnp.float32),
            jax.ShapeDtypeStruct((B * S, 2 * KV_WIDTH), jnp.float32),
            jax.ShapeDtypeStruct((B * S, NSA_WIDTH), jnp.float32),
        ),
        grid=(B, nt),
        in_specs=[rows(D_MODEL), const(g), const(w),
                  const(cw), const(cb), const(wr), const(br), const(wi), const(bi), const(lam),
                  tab, tab, tab, const(qg), const(kgs), const(kgw), const(ind)],
        out_specs=(rows(RG_WIDTH), tr(NSA_WIDTH), tok(KV_WIDTH), tr(KV_WIDTH), tok(KV_WIDTH), tr(KV_WIDTH),
                   tr(LANE), rows(2 * KV_WIDTH), rows(NSA_WIDTH)),
        scratch_shapes=[
            pltpu.VMEM((SUB, RG_WIDTH), jnp.float32),
            pltpu.VMEM((SUB, RG_WIDTH), jnp.float32),
            pltpu.VMEM((RG_TILE, RG_WIDTH), jnp.float32),
            pltpu.VMEM((RG_TILE, RG_WIDTH), jnp.float32),
        ],
        compiler_params=_cparams(("arbitrary", "arbitrary")),
        name="front",
    )(x2, g, w, cw, cb, wr, br, wi, bi, lam, ra, rm, rp, qg, kgs, kgw, ind)


N_CHUNK_TOK = CMP_STRIDE


def _compress_one(src_ref, wa_ref, wb_ref, pea_ref, peb_ref, w2_ref, n_chunk):
    ha = jnp.zeros((n_chunk, 2 * CMP_HIDDEN), jnp.float32)
    hb = jnp.zeros((n_chunk, 2 * CMP_HIDDEN), jnp.float32)
    for j in range(N_CHUNK_TOK):
        xj = src_ref[pl.ds(j, n_chunk, stride=N_CHUNK_TOK), :]
        ha = ha + jnp.dot((xj + pea_ref[j]).astype(jnp.bfloat16), wa_ref[j],
                          preferred_element_type=jnp.float32)
        hb = hb + jnp.dot((xj + peb_ref[j]).astype(jnp.bfloat16), wb_ref[j],
                          preferred_element_type=jnp.float32)
    hid = ha + pltpu.roll(hb, n_chunk - 1, axis=0)
    act = hid * jax.nn.sigmoid(hid)
    return jnp.dot(act.astype(jnp.bfloat16), w2_ref[...], preferred_element_type=jnp.float32)


def _compress_body(kc_ref, vc_ref, wak_ref, wbk_ref, peak_ref, pebk_ref, w2k_ref,
                   wav_ref, wbv_ref, peav_ref, pebv_ref, w2v_ref,
                   kg_ref, ind_ref, ra_ref, rm_ref, rp_ref, ko_ref, vto_ref, *, n_chunk):
    kc = _compress_one(kc_ref, wak_ref, wbk_ref, peak_ref, pebk_ref, w2k_ref, n_chunk)
    ko_ref[0] = _norm_rope(kc, kg_ref[...], ind_ref[...], ra_ref[...], rm_ref[...],
                           rp_ref[...]).astype(jnp.bfloat16)
    vc = _compress_one(vc_ref, wav_ref, wbv_ref, peav_ref, pebv_ref, w2v_ref, n_chunk)
    vto_ref[0] = vc.T.astype(jnp.bfloat16)


def _compress(z, B, S, wk, wv, kg, ind, ra, rm, rp):
    n_chunk = S // N_CHUNK_TOK

    def full(a):
        nd = a.ndim
        return pl.BlockSpec(a.shape, lambda b: (0,) * nd)

    consts = list(wk) + list(wv) + [kg, ind, ra, rm, rp]
    return pl.pallas_call(
        functools.partial(_compress_body, n_chunk=n_chunk),
        out_shape=(
            jax.ShapeDtypeStruct((B, n_chunk, KV_WIDTH), jnp.bfloat16),
            jax.ShapeDtypeStruct((B, KV_WIDTH, n_chunk), jnp.bfloat16),
        ),
        grid=(B,),
        in_specs=[
            pl.BlockSpec((S, KV_WIDTH), lambda b: (b, 0)),
            pl.BlockSpec((S, KV_WIDTH), lambda b: (b, 1)),
        ] + [full(a) for a in consts],
        out_specs=(
            pl.BlockSpec((1, n_chunk, KV_WIDTH), lambda b: (b, 0, 0)),
            pl.BlockSpec((1, KV_WIDTH, n_chunk), lambda b: (b, 0, 0)),
        ),
        compiler_params=_cparams(("arbitrary",)),
        name="compress",
    )(z, z, *consts)


G_COL = HPG * Q_BLOCK
N_COL = N_KV * G_COL


def _group_dots(vts, p):
    p16 = p.astype(jnp.bfloat16)
    return jnp.concatenate(
        [jnp.dot(vt, p16[:, g * G_COL:(g + 1) * G_COL], preferred_element_type=jnp.float32)
         for g, vt in enumerate(vts)], axis=1)


N_VARIANT = 8
MASK_ROWS = 128
ONES_ROWS = 16
ACC_ROWS = HEAD_DIM + ONES_ROWS


def _with_ones(vt):
    return jnp.concatenate([vt, jnp.ones((ONES_ROWS, vt.shape[1]), vt.dtype)], axis=0)


N_CBLK = N_COL // MXU_COLS


def _cblk(c):
    return slice(c * MXU_COLS, (c + 1) * MXU_COLS)


def _tile_softmax(blocks, vts):
    vones = [_with_ones(vt) for vt in vts]
    ms, accs = [], []
    for c in range(N_CBLK):
        sc = blocks(c)
        m = jnp.max(sc, axis=0, keepdims=True)
        p = jnp.exp2(sc - m).astype(jnp.bfloat16)
        ms.append(m)
        accs.append(jnp.dot(vones[c * MXU_COLS // G_COL], p, preferred_element_type=jnp.float32))
    return jnp.concatenate(ms, axis=1), jnp.concatenate(accs, axis=1)


def _merge(state, tile):
    m0, acc0 = state
    m1, acc1 = tile
    m = jnp.maximum(m0, m1)
    return m, jnp.exp2(m0 - m) * acc0 + jnp.exp2(m1 - m) * acc1


def _finish(state):
    _, acc = state
    return acc[0:HEAD_DIM] * (1.0 / acc[HEAD_DIM:HEAD_DIM + 1])


def _nsa_body(qt_ref, kc_ref, vct_ref, ks_ref, vst_ref, kw_ref, vwt_ref, gt_ref, ng0_ref, ng1_ref, e_ref,
              y_ref, ps_ref, bias_ref, sa_ref, sb_ref, *, n_cmp_pad):
    qi = pl.program_id(1)
    q0 = qi * Q_BLOCK
    groups = range(N_KV)
    grows = lambda g: slice(g * HEAD_DIM, (g + 1) * HEAD_DIM)

    qt = qt_ref[0]
    zero = jnp.zeros((HEAD_DIM, G_COL), jnp.bfloat16)
    qrows = []
    for g in groups:
        qcat = jnp.concatenate([qt[(g * HPG + h) * HEAD_DIM:(g * HPG + h + 1) * HEAD_DIM, :]
                                for h in range(HPG)], axis=1)
        qrows.append(jnp.concatenate([qcat if gg == g else zero for gg in groups], axis=1))
    qz = jnp.concatenate(qrows, axis=0)

    tl = lax.broadcasted_iota(jnp.int32, (1, N_COL), 1) % Q_BLOCK
    tcol = q0 + tl

    per = SEL_LEN // CMP_STRIDE
    n_sel_all = n_cmp_pad // per

    def compressed_and_topk(n_cmp):
        sc = jnp.dot(kc_ref[0, 0:n_cmp, :], qz, preferred_element_type=jnp.float32)
        n_open_rows = max(n_cmp - MASK_ROWS, 0)
        crow = n_open_rows + lax.broadcasted_iota(jnp.int32, (n_cmp - n_open_rows, N_COL), 0)
        cvalid = crow <= (tcol - (CMP_LEN - 1)) // CMP_STRIDE
        scm = jnp.concatenate(([sc[0:n_open_rows]] if n_open_rows else [])
                              + [jnp.where(cvalid, sc[n_open_rows:], NEG)], axis=0)
        mc = jnp.max(scm, axis=0, keepdims=True)
        pc = jnp.exp2(scm - mc)
        lc = jnp.sum(pc, axis=0, keepdims=True)
        pc = pc * jnp.where(tcol >= CMP_LEN - 1, 1.0 / lc, 0.0)
        o_c = _group_dots([vct_ref[0, grows(g), 0:n_cmp] for g in groups], pc)

        n_sel = n_cmp // per
        tq = q0 + lax.broadcasted_iota(jnp.int32, (n_sel, Q_BLOCK), 1)
        bt = tq // SEL_LEN
        rowj = lax.broadcasted_iota(jnp.int32, (n_sel, Q_BLOCK), 0)
        forced = (rowj == 0) | (rowj == bt) | (rowj == bt - 1)
        valid = rowj <= bt
        rowf = rowj.astype(jnp.float32)
        cands = []
        for g in groups:
            psum = pc[:, g * G_COL:g * G_COL + Q_BLOCK]
            for h in range(1, HPG):
                psum = psum + pc[:, g * G_COL + h * Q_BLOCK:g * G_COL + (h + 1) * Q_BLOCK]
            nl = Q_BLOCK // LANE
            for l in range(nl):
                ps_ref[g * nl + l, 0:SUB, :] = jnp.zeros((SUB, LANE), jnp.float32)
                ps_ref[g * nl + l, SUB:SUB + n_cmp, :] = psum[:, l * LANE:(l + 1) * LANE]
            strided = lambda k: jnp.concatenate(
                [ps_ref[g * nl + l, pl.ds(SUB + k, n_sel, stride=per), :] for l in range(nl)], axis=1)
            imp = strided(0) + strided(1) + strided(2) + 0.5 * strided(3) + 0.5 * strided(-1)
            cands.append(jnp.where(valid & jnp.logical_not(forced), imp, -1.0))

        eligible = [c >= 0.0 for c in cands]
        for _ in range(SEL_TOPN - 3):
            for g in groups:
                mx = jnp.max(cands[g], axis=0, keepdims=True)
                first = jnp.min(jnp.where(cands[g] == mx, rowf, float(n_sel)), axis=0, keepdims=True)
                cands[g] = jnp.where(rowf == first, -1.0, cands[g])
        taken = [(forced & valid) | (eligible[g] & (cands[g] < 0.0)) for g in groups]
        bias_ref[0:n_sel, :] = jnp.concatenate([jnp.where(tk, 0.0, NEG) for tk in taken], axis=1)
        if n_sel < n_sel_all:
            bias_ref[n_sel:n_sel_all, :] = jnp.full((n_sel_all - n_sel, N_KV * Q_BLOCK), NEG, jnp.float32)
        bias_ref[n_sel_all:, :] = jnp.zeros((LANE, N_KV * Q_BLOCK), jnp.float32)
        return o_c

    n_var = N_VARIANT
    quarter = n_cmp_pad // n_var
    assert MASK_ROWS >= quarter + Q_BLOCK // CMP_STRIDE
    variant = jnp.minimum((q0 + Q_BLOCK - CMP_LEN) // CMP_STRIDE // quarter, n_var - 1)
    o_c = lax.switch(variant, [functools.partial(compressed_and_topk, (v + 1) * quarter) for v in range(n_var)])

    e_mat = e_ref[...]

    def produce(dst_ref, t):
        koff = pl.multiple_of(t * SEL_TILE, SEL_TILE)
        boff = pl.multiple_of(t * SEL_BLOCKS, SEL_BLOCKS)
        brows = bias_ref[pl.ds(boff, LANE), :].astype(jnp.bfloat16)
        bcols = jnp.concatenate([brows[:, g * Q_BLOCK:(g + 1) * Q_BLOCK] for g in groups for _ in range(HPG)],
                                axis=1)
        rhs = jnp.concatenate([qz, bcols], axis=0)
        lhs = jnp.concatenate([ks_ref[0, pl.ds(koff, SEL_TILE), :], e_mat], axis=1)

        def block(c):
            dst_ref[c] = jnp.dot(lhs, rhs[:, _cblk(c)], preferred_element_type=jnp.float32)
        return block

    def sel_values(t):
        koff = pl.multiple_of(t * SEL_TILE, SEL_TILE)
        return [vst_ref[0, grows(g), pl.ds(koff, SEL_TILE)] for g in groups]

    def consume(s_ref, t, state, make_next=None, causal=False):
        srow = lax.broadcasted_iota(jnp.int32, (SEL_TILE, MXU_COLS), 0)
        limit = tcol - t * SEL_TILE

        def blocks(c):
            if make_next is not None:
                make_next(c)
            sc = s_ref[c]
            return jnp.where(srow <= limit[:, _cblk(c)], sc, NEG) if causal else sc
        return _merge(state, _tile_softmax(blocks, sel_values(t)))

    def pair(j, state):
        t = 2 * j
        state = consume(sa_ref, t, state, produce(sb_ref, t + 1))
        return consume(sb_ref, t + 1, state, produce(sa_ref, t + 2))

    n_open = q0 // SEL_TILE

    win_len = WINDOW + Q_BLOCK
    woff = pl.multiple_of(q0, Q_BLOCK)
    flag = jnp.where(lax.broadcasted_iota(jnp.int32, (LANE, N_COL), 0) == 0, NEG, 0.0)
    first_tile = produce(sa_ref, 0)
    for c in range(N_CBLK):
        first_tile(c)
    sw = jnp.dot(kw_ref[0, pl.ds(woff, win_len), :],
                 jnp.concatenate([qz, flag.astype(jnp.bfloat16)], axis=0),
                 preferred_element_type=jnp.float32)
    wrow = lax.broadcasted_iota(jnp.int32, (Q_BLOCK, N_COL), 0)
    sw = jnp.concatenate([jnp.where(wrow > tl, sw[0:Q_BLOCK], NEG)]
                         + ([sw[Q_BLOCK:WINDOW]] if Q_BLOCK < WINDOW else [])
                         + [jnp.where(wrow <= tl, sw[WINDOW:], NEG)], axis=0)
    o_w = _finish(_tile_softmax(lambda c: sw[:, _cblk(c)],
                                [vwt_ref[0, grows(g), pl.ds(woff, win_len)] for g in groups]))

    state = (jnp.full((1, N_COL), NEG, jnp.float32), jnp.zeros((ACC_ROWS, N_COL), jnp.float32))
    n_tail = jnp.where(n_open == 0, 0, 2 - n_open % 2)
    n_pair = (n_open - n_tail) // 2
    state = lax.fori_loop(0, n_pair // 2, lambda j, st: pair(2 * j + 1, pair(2 * j, st)), state)
    state = lax.fori_loop(2 * (n_pair // 2), n_pair, pair, state)
    t_next = 2 * n_pair

    def tail0(state):
        return _finish(consume(sa_ref, t_next, state, causal=True))

    def tail1(state):
        state = consume(sa_ref, t_next, state, produce(sb_ref, t_next + 1))
        return _finish(consume(sb_ref, t_next + 1, state, causal=True))

    def tail2(state):
        state = consume(sa_ref, t_next, state, produce(sb_ref, t_next + 1))
        state = consume(sb_ref, t_next + 1, state, produce(sa_ref, t_next + 2))
        return _finish(consume(sa_ref, t_next + 2, state, causal=True))

    o_s = lax.switch(n_tail, [tail0, tail1, tail2], state)

    def gate_row(branch):
        return jnp.concatenate(
            [gt_ref[0, r * N_BRANCH + branch:r * N_BRANCH + branch + 1, :] for r in range(N_KV * HPG)], axis=1)

    yt = gate_row(0) * o_c + gate_row(1) * o_s + gate_row(2) * o_w
    halves = []
    for pr in range(N_KV * HPG // 2):
        two = jnp.concatenate([yt[:, (2 * pr) * Q_BLOCK:(2 * pr + 1) * Q_BLOCK],
                               yt[:, (2 * pr + 1) * Q_BLOCK:(2 * pr + 2) * Q_BLOCK]], axis=0)
        halves.append(two.T)
    y = jnp.concatenate(halves, axis=1)
    ng = jnp.concatenate([ng0_ref[...], ng1_ref[...]], axis=1)
    y_ref[...] = (y * (ng * jax.nn.sigmoid(ng))).astype(y_ref.dtype)


def _nsa(z, qt, kc, vct, ks, vst, kw, vwt, gt, B, S):
    nq = S // Q_BLOCK
    n_cmp_pad = S // CMP_STRIDE
    gw = HPG * HEAD_DIM
    bf = jnp.bfloat16
    e_mat = (jnp.arange(SEL_TILE)[:, None] // SEL_LEN == jnp.arange(LANE)[None, :]).astype(bf)
    lead = jnp.zeros((B, WINDOW, 2 * KV_WIDTH), bf).at[:, :, KV_WIDTH].set(1.0)
    kw_pad = jnp.concatenate([lead, jnp.concatenate([kw, jnp.zeros_like(kw)], axis=2)], axis=1)
    vwt_pad = jnp.pad(vwt, ((0, 0), (0, 0), (WINDOW, 0)))
    whole = lambda a: pl.BlockSpec((1,) + a.shape[1:], lambda b, i: (b, 0, 0))
    return pl.pallas_call(
        functools.partial(_nsa_body, n_cmp_pad=n_cmp_pad),
        out_shape=jax.ShapeDtypeStruct((B * S, NSA_WIDTH), bf),
        grid=(B, nq),
        in_specs=[
            pl.BlockSpec((1, NSA_WIDTH, Q_BLOCK), lambda b, i: (b, 0, i)),
            whole(kc), whole(vct), whole(ks), whole(vst), whole(kw_pad), whole(vwt_pad),
            pl.BlockSpec((1, LANE, Q_BLOCK), lambda b, i: (b, 0, i)),
            pl.BlockSpec((Q_BLOCK, gw), lambda b, i: (b * nq + i, 0)),
            pl.BlockSpec((Q_BLOCK, gw), lambda b, i: (b * nq + i, 1)),
            pl.BlockSpec((SEL_TILE, LANE), lambda b, i: (0, 0)),
        ],
        out_specs=pl.BlockSpec((Q_BLOCK, NSA_WIDTH), lambda b, i: (b * nq + i, 0)),
        scratch_shapes=[
            pltpu.VMEM((N_KV * Q_BLOCK // LANE, SUB + n_cmp_pad, LANE), jnp.float32),
            pltpu.VMEM((S // SEL_LEN + LANE, N_KV * Q_BLOCK), jnp.float32),
            pltpu.VMEM((N_CBLK, SEL_TILE, MXU_COLS), jnp.float32),
            pltpu.VMEM((N_CBLK, SEL_TILE, MXU_COLS), jnp.float32),
        ],
        compiler_params=_cparams(("arbitrary", "arbitrary")),
        name="nsa",
    )(qt, kc, vct, ks, vst, kw_pad, vwt_pad, gt, z, z, e_mat)


def _out_proj_body(x_ref, ya_ref, yb_ref, wa_ref, wb_ref, o_ref):
    o_ref[...] = (x_ref[...]
                  + jnp.dot(ya_ref[...], wa_ref[...], preferred_element_type=jnp.float32)
                  + jnp.dot(yb_ref[...], wb_ref[...], preferred_element_type=jnp.float32))


def _out_proj(x2, ya, yb, wa, wb):
    rows = x2.shape[0]
    return pl.pallas_call(
        _out_proj_body,
        out_shape=jax.ShapeDtypeStruct((rows, D_MODEL), jnp.float32),
        grid=(rows // OUT_TILE,),
        in_specs=[
            pl.BlockSpec((OUT_TILE, D_MODEL), lambda i: (i, 0)),
            pl.BlockSpec((OUT_TILE, RG_WIDTH), lambda i: (i, 0)),
            pl.BlockSpec((OUT_TILE, NSA_WIDTH), lambda i: (i, 0)),
            pl.BlockSpec((RG_WIDTH, D_MODEL), lambda i: (0, 0)),
            pl.BlockSpec((NSA_WIDTH, D_MODEL), lambda i: (0, 0)),
        ],
        out_specs=pl.BlockSpec((OUT_TILE, D_MODEL), lambda i: (i, 0)),
        compiler_params=_cparams(("arbitrary",)),
        name="out_proj",
    )(x2, ya, yb, wa, wb)


def _rope_tables(pos):
    half = ROPE_DIM // 2
    inv = ROPE_THETA ** (-jnp.arange(half, dtype=jnp.float32) / half)
    ang = pos.astype(jnp.float32)[:, None] * inv[None, :]
    cos, sin = jnp.cos(ang), jnp.sin(ang)
    n = pos.shape[0]
    rest = HEAD_DIM - ROPE_DIM
    ra = jnp.concatenate([cos, cos, jnp.ones((n, rest), jnp.float32)], axis=1)
    rm = jnp.concatenate([-sin, jnp.zeros((n, half + rest), jnp.float32)], axis=1)
    rp = jnp.concatenate([jnp.zeros((n, half), jnp.float32), sin, jnp.zeros((n, rest), jnp.float32)], axis=1)
    return tuple(jnp.tile(t, (1, LANE // HEAD_DIM)) for t in (ra, rm, rp))


def _block_diag(blocks):
    n, r, c = blocks.shape
    eye = jnp.eye(n, dtype=blocks.dtype)
    return (eye[:, None, :, None] * blocks[:, :, None, :]).reshape(n * r, n * c)


def _compress_params(pe, w1, w2):
    bf = jnp.bfloat16
    w1t = w1.reshape(2, N_CHUNK_TOK, HEAD_DIM, CMP_HIDDEN)
    dup = lambda w: jax.vmap(lambda m: _block_diag(jnp.stack([m] * N_KV)))(w)
    wa, wb = dup(w1t[0]).astype(bf), dup(w1t[1]).astype(bf)
    pet = jnp.tile(pe.reshape(2, N_CHUNK_TOK, 1, HEAD_DIM), (1, 1, 1, N_KV))
    w2d = _block_diag(jnp.stack([w2] * N_KV)).astype(bf)
    return wa, wb, pet[0], pet[1], w2d


def _layer(x2, B, S, norm_g, w_in, conv_w, conv_b, rg_wr, rg_br, rg_wi, rg_bi, rg_lam,
           q_g, k_g, pe_k, w1_k, w2_k, pe_v, w1_v, w2_v, w_out, tabs_tok, tabs_cmp, ind):
    bf = jnp.bfloat16
    w_in_p = jnp.pad(w_in, ((0, 0), (0, N_PAD - N_IN))).astype(bf)
    tile2 = lambda v: jnp.tile(v, LANE // HEAD_DIM)[None, :]
    y_a, qt, ks, vst, kw, vwt, gt, kcvc, ng = _front(
        x2, B, S, norm_g[None, :], w_in_p,
        conv_w, conv_b[None, :], _block_diag(rg_wr).astype(bf), rg_br[None, :],
        _block_diag(rg_wi).astype(bf), rg_bi[None, :], rg_lam[None, :],
        *tabs_tok, tile2(q_g), tile2(k_g[1]), tile2(k_g[2]), ind)
    kc, vct = _compress(kcvc, B, S, _compress_params(pe_k, w1_k, w2_k), _compress_params(pe_v, w1_v, w2_v),
                        tile2(k_g[0]), ind, *tabs_cmp)
    y_b = _nsa(ng, qt, kc, vct, ks, vst, kw, vwt, gt, B, S)

    w_out16 = w_out.astype(bf)
    return _out_proj(x2, y_a, y_b, w_out16[:RG_WIDTH], w_out16[RG_WIDTH:])


def kernel(x, norm_g, w_in, conv_w, conv_b, rg_wr, rg_br, rg_wi, rg_bi, rg_lambda, q_norm_g, k_norm_g,
           cmp_pe_k, cmp_w1_k, cmp_w2_k, cmp_pe_v, cmp_w1_v, cmp_w2_v, w_out):
    B, S, D = x.shape
    assert D == D_MODEL and S % ROW_TILE == 0 and S % SEL_TILE == 0 and S >= WINDOW and (B * S) % OUT_TILE == 0
    depth = norm_g.shape[0]
    tabs_tok = _rope_tables(jnp.arange(S))
    tabs_cmp = _rope_tables(jnp.arange(S // CMP_STRIDE) * CMP_STRIDE + (CMP_LEN - 1))
    ind = _block_diag(jnp.ones((LANE // HEAD_DIM, HEAD_DIM, HEAD_DIM), jnp.bfloat16))
    x2 = x.reshape(B * S, D)
    for l in range(depth):
        x2 = _layer(x2, B, S, norm_g[l], w_in[l], conv_w[l], conv_b[l], rg_wr[l], rg_br[l], rg_wi[l],
                    rg_bi[l], rg_lambda[l], q_norm_g[l], k_norm_g[l], cmp_pe_k[l], cmp_w1_k[l],
                    cmp_w2_k[l], cmp_pe_v[l], cmp_w1_v[l], cmp_w2_v[l], w_out[l], tabs_tok, tabs_cmp, ind)
    return x2.reshape(B, S, D)
```

```python
import functools
import math

import jax
import jax.numpy as jnp
from jax import lax
from jax.experimental import pallas as pl
from jax.experimental.pallas import tpu as pltpu

D_MODEL = 1024
RG_WIDTH = 512
RG_BLOCKS = 8
RG_BLOCK = 64
CONV_W = 4
RG_C = 8.0
NSA_WIDTH = 512
HEAD_DIM = 64
N_KV = 2
HPG = 4
KV_WIDTH = N_KV * HEAD_DIM
ROPE_DIM = 16
ROPE_THETA = 500000.0
CMP_LEN = 32
CMP_STRIDE = 16
CMP_HIDDEN = 128
SEL_LEN = 64
SEL_TOPN = 16
WINDOW = 512
N_BRANCH = 3
Q_BLOCK = 256
EPS = 1e-6
NEG = -1e30
N_IN = 2 * RG_WIDTH + 2 * NSA_WIDTH + 6 * KV_WIDTH + N_BRANCH * 8
LANE = 128
MXU_COLS = 256
N_PAD = ((N_IN + LANE - 1) // LANE) * LANE

COL_Q = 2 * RG_WIDTH
COL_KC = COL_Q + NSA_WIDTH
COL_VC = COL_KC + KV_WIDTH
COL_KS = COL_VC + KV_WIDTH
COL_VS = COL_KS + KV_WIDTH
COL_KW = COL_VS + KV_WIDTH
COL_VW = COL_KW + KV_WIDTH
COL_NG = COL_VW + KV_WIDTH
COL_BG = COL_NG + NSA_WIDTH

VMEM_LIMIT = 56 * 1024 * 1024

ROW_TILE = 512
OUT_TILE = 2048
SEL_TILE = 1024
SEL_BLOCKS = SEL_TILE // SEL_LEN
LOG2E = 1.4426950408889634


def _cparams(sem):
    return pltpu.CompilerParams(dimension_semantics=sem, vmem_limit_bytes=VMEM_LIMIT)


def _head_mean_sq(x, ind):
    sq = x * x
    hi = sq.astype(jnp.bfloat16)
    lo = (sq - hi.astype(jnp.float32)).astype(jnp.bfloat16)
    tot = (jnp.dot(hi, ind, preferred_element_type=jnp.float32)
           + jnp.dot(lo, ind, preferred_element_type=jnp.float32))
    return tot * (1.0 / HEAD_DIM)


def _norm_rope(x, gain, ind, ra, rm, rp):
    width = x.shape[-1]
    y = (x * lax.rsqrt(_head_mean_sq(x, ind) + EPS)) * gain
    up = pltpu.roll(y, width - ROPE_DIM // 2, axis=1)
    dn = pltpu.roll(y, ROPE_DIM // 2, axis=1)
    return y * ra + up * rm + dn * rp


def _prep_q(zq, first, tabs, qg_ref, ind_ref, qt_ref):
    scale = LOG2E / math.sqrt(HEAD_DIM)
    for c in range(zq.shape[1] // LANE):
        qn = _norm_rope(zq[:, c * LANE:(c + 1) * LANE], qg_ref[...], ind_ref[...], *tabs) * scale
        qt_ref[0, (first + c) * LANE:(first + c + 1) * LANE, :] = qn.T.astype(jnp.bfloat16)


def _prep_kv(zkv, tabs, kg_ref, ind_ref, k_ref, vt_ref):
    k_ref[0] = _norm_rope(zkv[:, :KV_WIDTH], kg_ref[...], ind_ref[...], *tabs).astype(jnp.bfloat16)
    vt_ref[0] = zkv[:, KV_WIDTH:].T.astype(jnp.bfloat16)


RG_TILE = 512
RG_CHUNKS = 16
SUB = 8


def _block_diag_dot(x, w_ref):
    n = w_ref.shape[0] // MXU_COLS
    return jnp.concatenate(
        [jnp.dot(x[:, c * MXU_COLS:(c + 1) * MXU_COLS],
                 w_ref[c * MXU_COLS:(c + 1) * MXU_COLS, c * MXU_COLS:(c + 1) * MXU_COLS],
                 preferred_element_type=jnp.float32) for c in range(n)], axis=1)


def _rglru_tile(x, gate, cw_ref, cb_ref, wr_ref, br_ref, wi_ref, bi_ref, lam_ref,
                y_ref, tail_ref, carry_ref, a_ref, b_ref):
    tail = tail_ref[...]
    row8 = lax.broadcasted_iota(jnp.int32, (SUB, RG_WIDTH), 0)
    conv = x * cw_ref[CONV_W - 1:CONV_W, :] + cb_ref[...]
    for s in range(1, CONV_W):
        xs = pltpu.roll(x, s, axis=0)
        first = jnp.where(row8 < s, pltpu.roll(tail, s, axis=0), xs[0:SUB])
        xs = jnp.concatenate([first, xs[SUB:]], axis=0)
        conv = conv + xs * cw_ref[CONV_W - 1 - s:CONV_W - s, :]
    tail_ref[...] = x[RG_TILE - SUB:]
    yield

    softplus = jax.nn.softplus(-lam_ref[...])
    chunk = RG_TILE // RG_CHUNKS
    grp = (chunk // SUB, SUB, RG_WIDTH)
    row = lax.broadcasted_iota(jnp.int32, grp, 1)
    for c in range(RG_CHUNKS):
        rows = slice(c * chunk, (c + 1) * chunk)
        cv = conv[rows]
        cb16 = cv.astype(jnp.bfloat16)
        r = jax.nn.sigmoid(_block_diag_dot(cb16, wr_ref) + br_ref[...])
        ig = jax.nn.sigmoid(_block_diag_dot(cb16, wi_ref) + bi_ref[...])
        log_a = (-RG_C * r) * softplus
        a = jnp.exp(log_a)
        u = jnp.sqrt(-jnp.tanh(log_a) * (a * a + 1.0)) * (ig * cv)
        a, u = a.reshape(grp), u.reshape(grp)
        for d in (1, 2, 4):
            keep = row >= d
            a_s = jnp.where(keep, pltpu.roll(a, d, axis=1), 1.0)
            u_s = jnp.where(keep, pltpu.roll(u, d, axis=1), 0.0)
            u = a * u_s + u
            a = a * a_s
        a_ref[rows, :] = a.reshape(chunk, RG_WIDTH)
        b_ref[rows, :] = u.reshape(chunk, RG_WIDTH)
        yield

    h = carry_ref[...]
    for c in range(RG_CHUNKS):
        for k in range(c * chunk // SUB, (c + 1) * chunk // SUB):
            grows = slice(k * SUB, (k + 1) * SUB)
            hk = a_ref[grows, :] * h + b_ref[grows, :]
            b_ref[grows, :] = hk
            h = jnp.broadcast_to(hk[SUB - 1:SUB, :], (SUB, RG_WIDTH))
        yield
    carry_ref[...] = h

    for c in range(RG_CHUNKS):
        rows = slice(c * chunk, (c + 1) * chunk)
        gt = gate[rows]
        y_ref[rows, :] = (b_ref[rows, :] * (gt * jax.nn.sigmoid(gt))).astype(y_ref.dtype)
        yield


def _front_body(x_ref, g_ref, w_ref,
                cw_ref, cb_ref, wr_ref, br_ref, wi_ref, bi_ref, lam_ref,
                ra_ref, rm_ref, rp_ref, qg_ref, kgs_ref, kgw_ref, ind_ref,
                ya_ref, qt_ref, kso_ref, vst_ref, kwo_ref, vwt_ref, gt_ref, kcvc_ref, ng_ref,
                tail_ref, carry_ref, a_ref, b_ref):
    @pl.when(pl.program_id(1) == 0)
    def _():
        tail_ref[...] = jnp.zeros_like(tail_ref)
        carry_ref[...] = jnp.zeros_like(carry_ref)

    x = x_ref[...]
    ms = jnp.mean(x * x, axis=-1, keepdims=True)
    h = ((x * lax.rsqrt(ms + EPS)) * g_ref[...]).astype(jnp.bfloat16)
    proj = lambda c0, c1: jnp.dot(h, w_ref[:, c0:c1], preferred_element_type=jnp.float32)

    z_rg = proj(0, COL_Q)
    mixer = _rglru_tile(z_rg[:, :RG_WIDTH], z_rg[:, RG_WIDTH:], cw_ref, cb_ref, wr_ref, br_ref, wi_ref,
                        bi_ref, lam_ref, ya_ref, tail_ref, carry_ref, a_ref, b_ref)
    tabs = (ra_ref[...], rm_ref[...], rp_ref[...])
    half = 2 * KV_WIDTH
    z = {}

    def project(name, c0, c1):
        z[name] = proj(c0, c1)

    def store(ref, c0, c1):
        ref[...] = proj(c0, c1)

    def gates():
        gt_ref[0] = jax.nn.sigmoid(proj(COL_BG, N_PAD)).T

    items = [
        functools.partial(project, "q0", COL_Q, COL_Q + half),
        functools.partial(project, "q1", COL_Q + half, COL_KC),
        functools.partial(project, "sel", COL_KS, COL_KW),
        functools.partial(project, "win", COL_KW, COL_NG),
        functools.partial(store, kcvc_ref, COL_KC, COL_KS),
        lambda: _prep_q(z["q0"], 0, tabs, qg_ref, ind_ref, qt_ref),
        functools.partial(store, ng_ref.at[:, 0:half], COL_NG, COL_NG + half),
        lambda: _prep_q(z["q1"], half // LANE, tabs, qg_ref, ind_ref, qt_ref),
        functools.partial(store, ng_ref.at[:, half:], COL_NG + half, COL_BG),
        lambda: _prep_kv(z["sel"], tabs, kgs_ref, ind_ref, kso_ref, vst_ref),
        gates,
        lambda: _prep_kv(z["win"], tabs, kgw_ref, ind_ref, kwo_ref, vwt_ref),
    ]
    every = (1 + 3 * RG_CHUNKS) // len(items)
    for i, _ in enumerate(mixer):
        if items and i % every == 0:
            items.pop(0)()
    while items:
        items.pop(0)()


def _front(x2, B, S, g, w, cw, cb, wr, br, wi, bi, lam, ra, rm, rp, qg, kgs, kgw, ind):
    nt = S // ROW_TILE
    bf = jnp.bfloat16
    rows = lambda width: pl.BlockSpec((ROW_TILE, width), lambda b, i: (b * nt + i, 0))
    const = lambda a: pl.BlockSpec(a.shape, lambda b, i: (0, 0))
    tab = pl.BlockSpec((ROW_TILE, LANE), lambda b, i: (i, 0))
    tok = lambda width: pl.BlockSpec((1, ROW_TILE, width), lambda b, i: (b, i, 0))
    tr = lambda height: pl.BlockSpec((1, height, ROW_TILE), lambda b, i: (b, 0, i))
    return pl.pallas_call(
        _front_body,
        out_shape=(
            jax.ShapeDtypeStruct((B * S, RG_WIDTH), bf),
            jax.ShapeDtypeStruct((B, NSA_WIDTH, S), bf),
            jax.ShapeDtypeStruct((B, S, KV_WIDTH), bf),
            jax.ShapeDtypeStruct((B, KV_WIDTH, S), bf),
            jax.ShapeDtypeStruct((B, S, KV_WIDTH), bf),
            jax.ShapeDtypeStruct((B, KV_WIDTH, S), bf),
            jax.ShapeDtypeStruct((B, LANE, S), jnp.float32),
            jax.ShapeDtypeStruct((B * S, 2 * KV_WIDTH), jnp.float32),
            jax.ShapeDtypeStruct((B * S, NSA_WIDTH), jnp.float32),
        ),
        grid=(B, nt),
        in_specs=[rows(D_MODEL), const(g), const(w),
                  const(cw), const(cb), const(wr), const(br), const(wi), const(bi), const(lam),
                  tab, tab, tab, const(qg), const(kgs), const(kgw), const(ind)],
        out_specs=(rows(RG_WIDTH), tr(NSA_WIDTH), tok(KV_WIDTH), tr(KV_WIDTH), tok(KV_WIDTH), tr(KV_WIDTH),
                   tr(LANE), rows(2 * KV_WIDTH), rows(NSA_WIDTH)),
        scratch_shapes=[
            pltpu.VMEM((SUB, RG_WIDTH), jnp.float32),
            pltpu.VMEM((SUB, RG_WIDTH), jnp.float32),
            pltpu.VMEM((RG_TILE, RG_WIDTH), jnp.float32),
            pltpu.VMEM((RG_TILE, RG_WIDTH), jnp.float32),
        ],
        compiler_params=_cparams(("arbitrary", "arbitrary")),
        name="front",
    )(x2, g, w, cw, cb, wr, br, wi, bi, lam, ra, rm, rp, qg, kgs, kgw, ind)


N_CHUNK_TOK = CMP_STRIDE


def _compress_one(src_ref, wa_ref, wb_ref, pea_ref, peb_ref, w2_ref, n_chunk):
    ha = jnp.zeros((n_chunk, 2 * CMP_HIDDEN), jnp.float32)
    hb = jnp.zeros((n_chunk, 2 * CMP_HIDDEN), jnp.float32)
    for j in range(N_CHUNK_TOK):
        xj = src_ref[pl.ds(j, n_chunk, stride=N_CHUNK_TOK), :]
        ha = ha + jnp.dot((xj + pea_ref[j]).astype(jnp.bfloat16), wa_ref[j],
                          preferred_element_type=jnp.float32)
        hb = hb + jnp.dot((xj + peb_ref[j]).astype(jnp.bfloat16), wb_ref[j],
                          preferred_element_type=jnp.float32)
    hid = ha + pltpu.roll(hb, n_chunk - 1, axis=0)
    act = hid * jax.nn.sigmoid(hid)
    return jnp.dot(act.astype(jnp.bfloat16), w2_ref[...], preferred_element_type=jnp.float32)


def _compress_body(kc_ref, vc_ref, wak_ref, wbk_ref, peak_ref, pebk_ref, w2k_ref,
                   wav_ref, wbv_ref, peav_ref, pebv_ref, w2v_ref,
                   kg_ref, ind_ref, ra_ref, rm_ref, rp_ref, ko_ref, vto_ref, *, n_chunk):
    kc = _compress_one(kc_ref, wak_ref, wbk_ref, peak_ref, pebk_ref, w2k_ref, n_chunk)
    ko_ref[0] = _norm_rope(kc, kg_ref[...], ind_ref[...], ra_ref[...], rm_ref[...],
                           rp_ref[...]).astype(jnp.bfloat16)
    vc = _compress_one(vc_ref, wav_ref, wbv_ref, peav_ref, pebv_ref, w2v_ref, n_chunk)
    vto_ref[0] = vc.T.astype(jnp.bfloat16)


def _compress(z, B, S, wk, wv, kg, ind, ra, rm, rp):
    n_chunk = S // N_CHUNK_TOK

    def full(a):
        nd = a.ndim
        return pl.BlockSpec(a.shape, lambda b: (0,) * nd)

    consts = list(wk) + list(wv) + [kg, ind, ra, rm, rp]
    return pl.pallas_call(
        functools.partial(_compress_body, n_chunk=n_chunk),
        out_shape=(
            jax.ShapeDtypeStruct((B, n_chunk, KV_WIDTH), jnp.bfloat16),
            jax.ShapeDtypeStruct((B, KV_WIDTH, n_chunk), jnp.bfloat16),
        ),
        grid=(B,),
        in_specs=[
            pl.BlockSpec((S, KV_WIDTH), lambda b: (b, 0)),
            pl.BlockSpec((S, KV_WIDTH), lambda b: (b, 1)),
        ] + [full(a) for a in consts],
        out_specs=(
            pl.BlockSpec((1, n_chunk, KV_WIDTH), lambda b: (b, 0, 0)),
            pl.BlockSpec((1, KV_WIDTH, n_chunk), lambda b: (b, 0, 0)),
        ),
        compiler_params=_cparams(("arbitrary",)),
        name="compress",
    )(z, z, *consts)


G_COL = HPG * Q_BLOCK
N_COL = N_KV * G_COL


def _group_dots(vts, p):
    p16 = p.astype(jnp.bfloat16)
    return jnp.concatenate(
        [jnp.dot(vt, p16[:, g * G_COL:(g + 1) * G_COL], preferred_element_type=jnp.float32)
         for g, vt in enumerate(vts)], axis=1)


N_VARIANT = 8
MASK_ROWS = 128
ONES_ROWS = 16
ACC_ROWS = HEAD_DIM + ONES_ROWS


def _with_ones(vt):
    return jnp.concatenate([vt, jnp.ones((ONES_ROWS, vt.shape[1]), vt.dtype)], axis=0)


N_CBLK = N_COL // MXU_COLS


def _cblk(c):
    return slice(c * MXU_COLS, (c + 1) * MXU_COLS)


def _tile_softmax(blocks, vts):
    vones = [_with_ones(vt) for vt in vts]
    ms, accs = [], []
    for c in range(N_CBLK):
        sc = blocks(c)
        m = jnp.max(sc, axis=0, keepdims=True)
        p = jnp.exp2(sc - m).astype(jnp.bfloat16)
        ms.append(m)
        accs.append(jnp.dot(vones[c * MXU_COLS // G_COL], p, preferred_element_type=jnp.float32))
    return jnp.concatenate(ms, axis=1), jnp.concatenate(accs, axis=1)


def _merge(state, tile):
    m0, acc0 = state
    m1, acc1 = tile
    m = jnp.maximum(m0, m1)
    return m, jnp.exp2(m0 - m) * acc0 + jnp.exp2(m1 - m) * acc1


def _finish(state):
    _, acc = state
    return acc[0:HEAD_DIM] * (1.0 / acc[HEAD_DIM:HEAD_DIM + 1])


def _nsa_body(qt_ref, kc_ref, vct_ref, ks_ref, vst_ref, kw_ref, vwt_ref, gt_ref, ng0_ref, ng1_ref, e_ref,
              y_ref, ps_ref, bias_ref, sa_ref, sb_ref, *, n_cmp_pad):
    qi = pl.program_id(1)
    q0 = qi * Q_BLOCK
    groups = range(N_KV)
    grows = lambda g: slice(g * HEAD_DIM, (g + 1) * HEAD_DIM)

    qt = qt_ref[0]
    zero = jnp.zeros((HEAD_DIM, G_COL), jnp.bfloat16)
    qrows = []
    for g in groups:
        qcat = jnp.concatenate([qt[(g * HPG + h) * HEAD_DIM:(g * HPG + h + 1) * HEAD_DIM, :]
                                for h in range(HPG)], axis=1)
        qrows.append(jnp.concatenate([qcat if gg == g else zero for gg in groups], axis=1))
    qz = jnp.concatenate(qrows, axis=0)

    tl = lax.broadcasted_iota(jnp.int32, (1, N_COL), 1) % Q_BLOCK
    tcol = q0 + tl

    per = SEL_LEN // CMP_STRIDE
    n_sel_all = n_cmp_pad // per

    def compressed_and_topk(n_cmp):
        sc = jnp.dot(kc_ref[0, 0:n_cmp, :], qz, preferred_element_type=jnp.float32)
        n_open_rows = max(n_cmp - MASK_ROWS, 0)
        crow = n_open_rows + lax.broadcasted_iota(jnp.int32, (n_cmp - n_open_rows, N_COL), 0)
        cvalid = crow <= (tcol - (CMP_LEN - 1)) // CMP_STRIDE
        scm = jnp.concatenate(([sc[0:n_open_rows]] if n_open_rows else [])
                              + [jnp.where(cvalid, sc[n_open_rows:], NEG)], axis=0)
        mc = jnp.max(scm, axis=0, keepdims=True)
        pc = jnp.exp2(scm - mc)
        lc = jnp.sum(pc, axis=0, keepdims=True)
        pc = pc * jnp.where(tcol >= CMP_LEN - 1, 1.0 / lc, 0.0)
        o_c = _group_dots([vct_ref[0, grows(g), 0:n_cmp] for g in groups], pc)

        n_sel = n_cmp // per
        tq = q0 + lax.broadcasted_iota(jnp.int32, (n_sel, Q_BLOCK), 1)
        bt = tq // SEL_LEN
        rowj = lax.broadcasted_iota(jnp.int32, (n_sel, Q_BLOCK), 0)
        forced = (rowj == 0) | (rowj == bt) | (rowj == bt - 1)
        valid = rowj <= bt
        rowf = rowj.astype(jnp.float32)
        cands = []
        for g in groups:
            psum = pc[:, g * G_COL:g * G_COL + Q_BLOCK]
            for h in range(1, HPG):
                psum = psum + pc[:, g * G_COL + h * Q_BLOCK:g * G_COL + (h + 1) * Q_BLOCK]
            nl = Q_BLOCK // LANE
            for l in range(nl):
                ps_ref[g * nl + l, 0:SUB, :] = jnp.zeros((SUB, LANE), jnp.float32)
                ps_ref[g * nl + l, SUB:SUB + n_cmp, :] = psum[:, l * LANE:(l + 1) * LANE]
            strided = lambda k: jnp.concatenate(
                [ps_ref[g * nl + l, pl.ds(SUB + k, n_sel, stride=per), :] for l in range(nl)], axis=1)
            imp = strided(0) + strided(1) + strided(2) + 0.5 * strided(3) + 0.5 * strided(-1)
            cands.append(jnp.where(valid & jnp.logical_not(forced), imp, -1.0))

        eligible = [c >= 0.0 for c in cands]
        for _ in range(SEL_TOPN - 3):
            for g in groups:
                mx = jnp.max(cands[g], axis=0, keepdims=True)
                first = jnp.min(jnp.where(cands[g] == mx, rowf, float(n_sel)), axis=0, keepdims=True)
                cands[g] = jnp.where(rowf == first, -1.0, cands[g])
        taken = [(forced & valid) | (eligible[g] & (cands[g] < 0.0)) for g in groups]
        bias_ref[0:n_sel, :] = jnp.concatenate([jnp.where(tk, 0.0, NEG) for tk in taken], axis=1)
        if n_sel < n_sel_all:
            bias_ref[n_sel:n_sel_all, :] = jnp.full((n_sel_all - n_sel, N_KV * Q_BLOCK), NEG, jnp.float32)
        bias_ref[n_sel_all:, :] = jnp.zeros((LANE, N_KV * Q_BLOCK), jnp.float32)
        return o_c

    n_var = N_VARIANT
    quarter = n_cmp_pad // n_var
    assert MASK_ROWS >= quarter + Q_BLOCK // CMP_STRIDE
    variant = jnp.minimum((q0 + Q_BLOCK - CMP_LEN) // CMP_STRIDE // quarter, n_var - 1)
    o_c = lax.switch(variant, [functools.partial(compressed_and_topk, (v + 1) * quarter) for v in range(n_var)])

    e_mat = e_ref[...]

    def produce(dst_ref, t):
        koff = pl.multiple_of(t * SEL_TILE, SEL_TILE)
        boff = pl.multiple_of(t * SEL_BLOCKS, SEL_BLOCKS)
        brows = bias_ref[pl.ds(boff, LANE), :].astype(jnp.bfloat16)
        bcols = jnp.concatenate([brows[:, g * Q_BLOCK:(g + 1) * Q_BLOCK] for g in groups for _ in range(HPG)],
                                axis=1)
        rhs = jnp.concatenate([qz, bcols], axis=0)
        lhs = jnp.concatenate([ks_ref[0, pl.ds(koff, SEL_TILE), :], e_mat], axis=1)

        def block(c):
            dst_ref[:, _cblk(c)] = jnp.dot(lhs, rhs[:, _cblk(c)], preferred_element_type=jnp.float32)
        return block

    def sel_values(t):
        koff = pl.multiple_of(t * SEL_TILE, SEL_TILE)
        return [vst_ref[0, grows(g), pl.ds(koff, SEL_TILE)] for g in groups]

    def consume(s_ref, t, state, make_next=None, causal=False):
        srow = lax.broadcasted_iota(jnp.int32, (SEL_TILE, MXU_COLS), 0)
        limit = tcol - t * SEL_TILE

        def blocks(c):
            if make_next is not None:
                make_next(c)
            sc = s_ref[:, _cblk(c)]
            return jnp.where(srow <= limit[:, _cblk(c)], sc, NEG) if causal else sc
        return _merge(state, _tile_softmax(blocks, sel_values(t)))

    def pair(j, state):
        t = 2 * j
        state = consume(sa_ref, t, state, produce(sb_ref, t + 1))
        return consume(sb_ref, t + 1, state, produce(sa_ref, t + 2))

    n_open = q0 // SEL_TILE

    win_len = WINDOW + Q_BLOCK
    woff = pl.multiple_of(q0, Q_BLOCK)
    flag = jnp.where(lax.broadcasted_iota(jnp.int32, (LANE, N_COL), 0) == 0, NEG, 0.0)
    first_tile = produce(sa_ref, 0)
    for c in range(N_CBLK):
        first_tile(c)
    sw = jnp.dot(kw_ref[0, pl.ds(woff, win_len), :],
                 jnp.concatenate([qz, flag.astype(jnp.bfloat16)], axis=0),
                 preferred_element_type=jnp.float32)
    wrow = lax.broadcasted_iota(jnp.int32, (Q_BLOCK, N_COL), 0)
    sw = jnp.concatenate([jnp.where(wrow > tl, sw[0:Q_BLOCK], NEG)]
                         + ([sw[Q_BLOCK:WINDOW]] if Q_BLOCK < WINDOW else [])
                         + [jnp.where(wrow <= tl, sw[WINDOW:], NEG)], axis=0)
    o_w = _finish(_tile_softmax(lambda c: sw[:, _cblk(c)],
                                [vwt_ref[0, grows(g), pl.ds(woff, win_len)] for g in groups]))

    state = (jnp.full((1, N_COL), NEG, jnp.float32), jnp.zeros((ACC_ROWS, N_COL), jnp.float32))
    n_tail = jnp.where(n_open == 0, 0, 2 - n_open % 2)
    n_pair = (n_open - n_tail) // 2
    state = lax.fori_loop(0, n_pair // 2, lambda j, st: pair(2 * j + 1, pair(2 * j, st)), state)
    state = lax.fori_loop(2 * (n_pair // 2), n_pair, pair, state)
    t_next = 2 * n_pair

    def tail0(state):
        return _finish(consume(sa_ref, t_next, state, causal=True))

    def tail1(state):
        state = consume(sa_ref, t_next, state, produce(sb_ref, t_next + 1))
        return _finish(consume(sb_ref, t_next + 1, state, causal=True))

    def tail2(state):
        state = consume(sa_ref, t_next, state, produce(sb_ref, t_next + 1))
        state = consume(sb_ref, t_next + 1, state, produce(sa_ref, t_next + 2))
        return _finish(consume(sa_ref, t_next + 2, state, causal=True))

    o_s = lax.switch(n_tail, [tail0, tail1, tail2], state)

    def gate_row(branch):
        return jnp.concatenate(
            [gt_ref[0, r * N_BRANCH + branch:r * N_BRANCH + branch + 1, :] for r in range(N_KV * HPG)], axis=1)

    yt = gate_row(0) * o_c + gate_row(1) * o_s + gate_row(2) * o_w
    halves = []
    for pr in range(N_KV * HPG // 2):
        two = jnp.concatenate([yt[:, (2 * pr) * Q_BLOCK:(2 * pr + 1) * Q_BLOCK],
                               yt[:, (2 * pr + 1) * Q_BLOCK:(2 * pr + 2) * Q_BLOCK]], axis=0)
        halves.append(two.T)
    y = jnp.concatenate(halves, axis=1)
    ng = jnp.concatenate([ng0_ref[...], ng1_ref[...]], axis=1)
    y_ref[...] = (y * (ng * jax.nn.sigmoid(ng))).astype(y_ref.dtype)


def _nsa(z, qt, kc, vct, ks, vst, kw, vwt, gt, B, S):
    nq = S // Q_BLOCK
    n_cmp_pad = S // CMP_STRIDE
    gw = HPG * HEAD_DIM
    bf = jnp.bfloat16
    e_mat = (jnp.arange(SEL_TILE)[:, None] // SEL_LEN == jnp.arange(LANE)[None, :]).astype(bf)
    lead = jnp.zeros((B, WINDOW, 2 * KV_WIDTH), bf).at[:, :, KV_WIDTH].set(1.0)
    kw_pad = jnp.concatenate([lead, jnp.concatenate([kw, jnp.zeros_like(kw)], axis=2)], axis=1)
    vwt_pad = jnp.pad(vwt, ((0, 0), (0, 0), (WINDOW, 0)))
    whole = lambda a: pl.BlockSpec((1,) + a.shape[1:], lambda b, i: (b, 0, 0))
    return pl.pallas_call(
        functools.partial(_nsa_body, n_cmp_pad=n_cmp_pad),
        out_shape=jax.ShapeDtypeStruct((B * S, NSA_WIDTH), bf),
        grid=(B, nq),
        in_specs=[
            pl.BlockSpec((1, NSA_WIDTH, Q_BLOCK), lambda b, i: (b, 0, i)),
            whole(kc), whole(vct), whole(ks), whole(vst), whole(kw_pad), whole(vwt_pad),
            pl.BlockSpec((1, LANE, Q_BLOCK), lambda b, i: (b, 0, i)),
            pl.BlockSpec((Q_BLOCK, gw), lambda b, i: (b * nq + i, 0)),
            pl.BlockSpec((Q_BLOCK, gw), lambda b, i: (b * nq + i, 1)),
            pl.BlockSpec((SEL_TILE, LANE), lambda b, i: (0, 0)),
        ],
        out_specs=pl.BlockSpec((Q_BLOCK, NSA_WIDTH), lambda b, i: (b * nq + i, 0)),
        scratch_shapes=[
            pltpu.VMEM((N_KV * Q_BLOCK // LANE, SUB + n_cmp_pad, LANE), jnp.float32),
            pltpu.VMEM((S // SEL_LEN + LANE, N_KV * Q_BLOCK), jnp.float32),
            pltpu.VMEM((SEL_TILE, N_COL), jnp.float32),
            pltpu.VMEM((SEL_TILE, N_COL), jnp.float32),
        ],
        compiler_params=_cparams(("arbitrary", "arbitrary")),
        name="nsa",
    )(qt, kc, vct, ks, vst, kw_pad, vwt_pad, gt, z, z, e_mat)


def _out_proj_body(x_ref, ya_ref, yb_ref, wa_ref, wb_ref, o_ref):
    o_ref[...] = (x_ref[...]
                  + jnp.dot(ya_ref[...], wa_ref[...], preferred_element_type=jnp.float32)
                  + jnp.dot(yb_ref[...], wb_ref[...], preferred_element_type=jnp.float32))


def _out_proj(x2, ya, yb, wa, wb):
    rows = x2.shape[0]
    return pl.pallas_call(
        _out_proj_body,
        out_shape=jax.ShapeDtypeStruct((rows, D_MODEL), jnp.float32),
        grid=(rows // OUT_TILE,),
        in_specs=[
            pl.BlockSpec((OUT_TILE, D_MODEL), lambda i: (i, 0)),
            pl.BlockSpec((OUT_TILE, RG_WIDTH), lambda i: (i, 0)),
            pl.BlockSpec((OUT_TILE, NSA_WIDTH), lambda i: (i, 0)),
            pl.BlockSpec((RG_WIDTH, D_MODEL), lambda i: (0, 0)),
            pl.BlockSpec((NSA_WIDTH, D_MODEL), lambda i: (0, 0)),
        ],
        out_specs=pl.BlockSpec((OUT_TILE, D_MODEL), lambda i: (i, 0)),
        compiler_params=_cparams(("arbitrary",)),
        name="out_proj",
    )(x2, ya, yb, wa, wb)


def _rope_tables(pos):
    half = ROPE_DIM // 2
    inv = ROPE_THETA ** (-jnp.arange(half, dtype=jnp.float32) / half)
    ang = pos.astype(jnp.float32)[:, None] * inv[None, :]
    cos, sin = jnp.cos(ang), jnp.sin(ang)
    n = pos.shape[0]
    rest = HEAD_DIM - ROPE_DIM
    ra = jnp.concatenate([cos, cos, jnp.ones((n, rest), jnp.float32)], axis=1)
    rm = jnp.concatenate([-sin, jnp.zeros((n, half + rest), jnp.float32)], axis=1)
    rp = jnp.concatenate([jnp.zeros((n, half), jnp.float32), sin, jnp.zeros((n, rest), jnp.float32)], axis=1)
    return tuple(jnp.tile(t, (1, LANE // HEAD_DIM)) for t in (ra, rm, rp))


def _block_diag(blocks):
    n, r, c = blocks.shape
    eye = jnp.eye(n, dtype=blocks.dtype)
    return (eye[:, None, :, None] * blocks[:, :, None, :]).reshape(n * r, n * c)


def _compress_params(pe, w1, w2):
    bf = jnp.bfloat16
    w1t = w1.reshape(2, N_CHUNK_TOK, HEAD_DIM, CMP_HIDDEN)
    dup = lambda w: jax.vmap(lambda m: _block_diag(jnp.stack([m] * N_KV)))(w)
    wa, wb = dup(w1t[0]).astype(bf), dup(w1t[1]).astype(bf)
    pet = jnp.tile(pe.reshape(2, N_CHUNK_TOK, 1, HEAD_DIM), (1, 1, 1, N_KV))
    w2d = _block_diag(jnp.stack([w2] * N_KV)).astype(bf)
    return wa, wb, pet[0], pet[1], w2d


def _layer(x2, B, S, norm_g, w_in, conv_w, conv_b, rg_wr, rg_br, rg_wi, rg_bi, rg_lam,
           q_g, k_g, pe_k, w1_k, w2_k, pe_v, w1_v, w2_v, w_out, tabs_tok, tabs_cmp, ind):
    bf = jnp.bfloat16
    w_in_p = jnp.pad(w_in, ((0, 0), (0, N_PAD - N_IN))).astype(bf)
    tile2 = lambda v: jnp.tile(v, LANE // HEAD_DIM)[None, :]
    y_a, qt, ks, vst, kw, vwt, gt, kcvc, ng = _front(
        x2, B, S, norm_g[None, :], w_in_p,
        conv_w, conv_b[None, :], _block_diag(rg_wr).astype(bf), rg_br[None, :],
        _block_diag(rg_wi).astype(bf), rg_bi[None, :], rg_lam[None, :],
        *tabs_tok, tile2(q_g), tile2(k_g[1]), tile2(k_g[2]), ind)
    kc, vct = _compress(kcvc, B, S, _compress_params(pe_k, w1_k, w2_k), _compress_params(pe_v, w1_v, w2_v),
                        tile2(k_g[0]), ind, *tabs_cmp)
    y_b = _nsa(ng, qt, kc, vct, ks, vst, kw, vwt, gt, B, S)

    w_out16 = w_out.astype(bf)
    return _out_proj(x2, y_a, y_b, w_out16[:RG_WIDTH], w_out16[RG_WIDTH:])


def kernel(x, norm_g, w_in, conv_w, conv_b, rg_wr, rg_br, rg_wi, rg_bi, rg_lambda, q_norm_g, k_norm_g,
           cmp_pe_k, cmp_w1_k, cmp_w2_k, cmp_pe_v, cmp_w1_v, cmp_w2_v, w_out):
    B, S, D = x.shape
    assert D == D_MODEL and S % ROW_TILE == 0 and S % SEL_TILE == 0 and S >= WINDOW and (B * S) % OUT_TILE == 0
    depth = norm_g.shape[0]
    tabs_tok = _rope_tables(jnp.arange(S))
    tabs_cmp = _rope_tables(jnp.arange(S // CMP_STRIDE) * CMP_STRIDE + (CMP_LEN - 1))
    ind = _block_diag(jnp.ones((LANE // HEAD_DIM, HEAD_DIM, HEAD_DIM), jnp.bfloat16))
    x2 = x.reshape(B * S, D)
    for l in range(depth):
        x2 = _layer(x2, B, S, norm_g[l], w_in[l], conv_w[l], conv_b[l], rg_wr[l], rg_br[l], rg_wi[l],
                    rg_bi[l], rg_lambda[l], q_norm_g[l], k_norm_g[l], cmp_pe_k[l], cmp_w1_k[l],
                    cmp_w2_k[l], cmp_pe_v[l], cmp_w1_v[l], cmp_w2_v[l], w_out[l], tabs_tok, tabs_cmp, ind)
    return x2.reshape(B, S, D)
```
